```python
import math
import jax, jax.numpy as jnp
from jax import lax
import numpy as np

D_MODEL = 1024
BATCH = 8
SEQ = 4096
DEPTH = 4

GRID_W = 64
NA_HEADS = 8
NA_HEAD_DIM = 64
NA_WIDTH = NA_HEADS * NA_HEAD_DIM
NA_WIN_ROWS = 8
NA_WIN_COLS = 16
NA_COL_BLOCK = NA_WIN_COLS
NA_KEY_COL_BLOCK = 2 * NA_WIN_COLS
DIFF_HEADS = 4
DIFF_HEAD_DIM = 64
DIFF_QK_WIDTH = DIFF_HEADS * 2 * DIFF_HEAD_DIM
DIFF_V_WIDTH = DIFF_HEADS * 2 * DIFF_HEAD_DIM
MLA_HEADS = 8
MLA_NOPE_DIM = 64
MLA_ROPE_DIM = 32
MLA_V_DIM = 64
MLA_Q_RANK = 384
MLA_KV_RANK = 256
MLA_WIDTH = MLA_HEADS * MLA_V_DIM
N_BRANCH = 3
BRANCH_WIDTH = 512
D_IN = 3 * NA_WIDTH + 2 * DIFF_QK_WIDTH + DIFF_V_WIDTH + MLA_Q_RANK + MLA_KV_RANK + MLA_ROPE_DIM + N_BRANCH * D_MODEL
D_FF = -(-8 * D_MODEL // (3 * 256)) * 256
ROPE_THETA = 500000.0
DIFF_ROT_DIM = DIFF_HEAD_DIM // 4
Q_BLOCK = 128
DN_ALPHA = (2 * DEPTH) ** 0.25
DN_BETA = (8 * DEPTH) ** -0.25
LN_EPS = 1e-5
RMS_EPS = 1e-6

kernel_name = "hybrid_na_diff_mla_gated_deepnorm_encoder"


def layer_norm(x, g, b):
    xf = x.astype(jnp.float32)
    mu = jnp.mean(xf, -1, keepdims=True)
    var = jnp.mean(jnp.square(xf - mu), -1, keepdims=True)
    return ((xf - mu) * lax.rsqrt(var + LN_EPS) * g.astype(jnp.float32) + b.astype(jnp.float32)).astype(x.dtype)


def rms_norm(x, g):
    xf = x.astype(jnp.float32)
    return (xf * lax.rsqrt(jnp.mean(xf * xf, -1, keepdims=True) + RMS_EPS) * g.astype(jnp.float32)).astype(x.dtype)


def rope(x, rot_dim):
    s = x.shape[1]
    half = rot_dim // 2
    inv_freq = jnp.exp(-math.log(ROPE_THETA) * jnp.arange(half, dtype=jnp.float32) / half)
    ang = jnp.arange(s, dtype=jnp.float32)[:, None] * inv_freq[None, :]
    cos = jnp.cos(ang)[None, :, None, :]
    sin = jnp.sin(ang)[None, :, None, :]
    xr = x[..., :rot_dim].astype(jnp.float32)
    x1, x2 = xr[..., :half], xr[..., half:]
    rot = jnp.concatenate([x1 * cos - x2 * sin, x2 * cos + x1 * sin], -1).astype(x.dtype)
    return jnp.concatenate([rot, x[..., rot_dim:]], -1)


def blocked_attention(q, k, v, scale):
    b, s, h, dq = q.shape
    nb = s // Q_BLOCK
    qb = q.reshape(b, nb, Q_BLOCK, h, dq).transpose(1, 0, 2, 3, 4)

    def one_block(qi):
        sc = jnp.einsum('bqhd,bkhd->bhqk', qi, k, preferred_element_type=jnp.float32) * scale
        p = jax.nn.softmax(sc, axis=-1).astype(v.dtype)
        return jnp.einsum('bhqk,bkhd->bqhd', p, v)

    out = lax.map(one_block, qb)
    return out.transpose(1, 0, 2, 3, 4).reshape(b, s, h, v.shape[-1])


def neighbourhood_attention(q, k, v, rpb):
    b, s, h, d = q.shape
    rows = s // GRID_W
    kr = min(NA_WIN_ROWS, rows)
    kc = NA_WIN_COLS
    ncb = GRID_W // NA_COL_BLOCK
    qg = q.reshape(b, rows, GRID_W, h, d)
    kg = k.reshape(b, rows, GRID_W, h, d)
    vg = v.reshape(b, rows, GRID_W, h, d)
    qcol = jnp.arange(GRID_W).reshape(ncb, NA_COL_BLOCK)
    kstart = jnp.clip(jnp.arange(ncb) * NA_COL_BLOCK - kc // 2, 0, GRID_W - NA_KEY_COL_BLOCK)
    kcol = kstart[:, None] + jnp.arange(NA_KEY_COL_BLOCK)[None, :]
    wstart = jnp.clip(qcol - kc // 2, 0, GRID_W - kc)
    col_valid = (kcol[:, None, :] >= wstart[:, :, None]) & (kcol[:, None, :] < wstart[:, :, None] + kc)
    dcol_idx = jnp.clip(kcol[:, None, :] - qcol[:, :, None], -(kc - 1), kc - 1) + (kc - 1)
    row_center = NA_WIN_ROWS - 1
    scale = d ** -0.5

    def one_row(r):
        rs = jnp.clip(r - kr // 2, 0, rows - kr)
        qr = lax.dynamic_index_in_dim(qg, r, axis=1, keepdims=False)
        k_rows = lax.dynamic_slice_in_dim(kg, rs, kr, axis=1)
        v_rows = lax.dynamic_slice_in_dim(vg, rs, kr, axis=1)
        kb = jnp.take(k_rows, kcol, axis=2)
        vb = jnp.take(v_rows, kcol, axis=2)
        qb = qr.reshape(b, ncb, NA_COL_BLOCK, h, d)
        sc = jnp.einsum('bnqhd,banchd->bhnqac', qb, kb, preferred_element_type=jnp.float32) * scale
        drow = rs + jnp.arange(kr) - r + row_center
        bias = rpb[:, drow][:, :, dcol_idx].transpose(0, 2, 3, 1, 4)
        sc = sc + bias[None].astype(jnp.float32)
        sc = jnp.where(col_valid[None, None, :, :, None, :], sc, -jnp.inf)
        p = jax.nn.softmax(sc, axis=(-2, -1)).astype(v.dtype)
        o = jnp.einsum('bhnqac,banchd->bnqhd', p, vb)
        return o.reshape(b, GRID_W, h, d)

    out = lax.map(one_row, jnp.arange(rows))
    return out.transpose(1, 0, 2, 3, 4).reshape(b, s, h * d)


def diff_attention(q, k, v, lam_vecs, subln_g, lam_init):
    b, s, _ = q.shape
    q = q.reshape(b, s, DIFF_HEADS, 2, DIFF_HEAD_DIM)
    k = k.reshape(b, s, DIFF_HEADS, 2, DIFF_HEAD_DIM)
    v = v.reshape(b, s, DIFF_HEADS, 2 * DIFF_HEAD_DIM)
    q1, q2 = rope(q[:, :, :, 0], DIFF_ROT_DIM), rope(q[:, :, :, 1], DIFF_ROT_DIM)
    k1, k2 = rope(k[:, :, :, 0], DIFF_ROT_DIM), rope(k[:, :, :, 1], DIFF_ROT_DIM)
    lv = lam_vecs.astype(jnp.float32)
    lam = jnp.exp(jnp.sum(lv[0] * lv[1])) - jnp.exp(jnp.sum(lv[2] * lv[3])) + lam_init
    scale = DIFF_HEAD_DIM ** -0.5
    a1 = blocked_attention(q1, k1, v, scale)
    a2 = blocked_attention(q2, k2, v, scale)
    o = a1 - lam.astype(a1.dtype) * a2
    o = rms_norm(o, subln_g) * (1.0 - lam_init)
    return o.reshape(b, s, DIFF_V_WIDTH)


def latent_attention(c_q, c_kv, k_rope, w_qb, w_kvb, q_norm_g, kv_norm_g):
    b, s, _ = c_q.shape
    q = jnp.einsum('bsr,re->bse', rms_norm(c_q, q_norm_g), w_qb).reshape(b, s, MLA_HEADS, MLA_NOPE_DIM + MLA_ROPE_DIM)
    q = jnp.concatenate([q[..., :MLA_NOPE_DIM], rope(q[..., MLA_NOPE_DIM:], MLA_ROPE_DIM)], -1)
    kv = jnp.einsum('bsr,re->bse', rms_norm(c_kv, kv_norm_g), w_kvb).reshape(b, s, MLA_HEADS, MLA_NOPE_DIM + MLA_V_DIM)
    k_nope, v = kv[..., :MLA_NOPE_DIM], kv[..., MLA_NOPE_DIM:]
    k_r = rope(k_rope[:, :, None, :], MLA_ROPE_DIM)
    k = jnp.concatenate([k_nope, jnp.broadcast_to(k_r, (b, s, MLA_HEADS, MLA_ROPE_DIM))], -1)
    o = blocked_attention(q, k, v, (MLA_NOPE_DIM + MLA_ROPE_DIM) ** -0.5)
    return o.reshape(b, s, MLA_WIDTH)


def setup_inputs(seed: int = 0) -> dict:
    key = jax.random.key(seed)
    ks = jax.random.split(key, 20)

    def nrm(k, shape, s):
        return jax.random.normal(k, shape, jnp.float32) * s

    return {
        "x": nrm(ks[0], (BATCH, SEQ, D_MODEL), 1.0),
        "ln_in_g": 1.0 + nrm(ks[1], (D_MODEL,), 0.02),
        "ln_in_b": nrm(ks[2], (D_MODEL,), 0.02),
        "w_in": nrm(ks[3], (DEPTH, D_MODEL, D_IN), D_MODEL ** -0.5),
        "b_gate": nrm(ks[4], (DEPTH, N_BRANCH * D_MODEL), 0.02),
        "na_rpb": nrm(ks[5], (DEPTH, NA_HEADS, 2 * NA_WIN_ROWS - 1, 2 * NA_WIN_COLS - 1), 0.02),
        "diff_lambda": nrm(ks[6], (DEPTH, 4, DIFF_HEAD_DIM), 0.1),
        "diff_subln_g": 1.0 + nrm(ks[7], (DEPTH, 2 * DIFF_HEAD_DIM), 0.02),
        "mla_q_norm_g": 1.0 + nrm(ks[8], (DEPTH, MLA_Q_RANK), 0.02),
        "mla_kv_norm_g": 1.0 + nrm(ks[9], (DEPTH, MLA_KV_RANK), 0.02),
        "w_mla_qb": nrm(ks[10], (DEPTH, MLA_Q_RANK, MLA_HEADS * (MLA_NOPE_DIM + MLA_ROPE_DIM)), MLA_Q_RANK ** -0.5),
        "w_mla_kvb": nrm(ks[11], (DEPTH, MLA_KV_RANK, MLA_HEADS * (MLA_NOPE_DIM + MLA_V_DIM)), MLA_KV_RANK ** -0.5),
        "w_branch": nrm(ks[12], (DEPTH, N_BRANCH, BRANCH_WIDTH, D_MODEL), BRANCH_WIDTH ** -0.5 * DN_BETA),
        "w_out": nrm(ks[13], (DEPTH, D_MODEL, D_MODEL), D_MODEL ** -0.5 * DN_BETA),
        "ln1_g": 1.0 + nrm(ks[14], (DEPTH, D_MODEL), 0.02),
        "ln1_b": nrm(ks[15], (DEPTH, D_MODEL), 0.02),
        "w_ffn_in": nrm(ks[16], (DEPTH, D_MODEL, 2 * D_FF), D_MODEL ** -0.5),
        "w_ffn_out": nrm(ks[17], (DEPTH, D_FF, D_MODEL), D_FF ** -0.5 * DN_BETA),
        "ln2_g": 1.0 + nrm(ks[18], (DEPTH, D_MODEL), 0.02),
        "ln2_b": nrm(ks[19], (DEPTH, D_MODEL), 0.02),
    }


def reference(x, ln_in_g, ln_in_b, w_in, b_gate, na_rpb, diff_lambda, diff_subln_g,
              mla_q_norm_g, mla_kv_norm_g, w_mla_qb, w_mla_kvb, w_branch, w_out,
              ln1_g, ln1_b, w_ffn_in, w_ffn_out, ln2_g, ln2_b):
    b, s, _ = x.shape
    sizes = (NA_WIDTH, NA_WIDTH, NA_WIDTH, DIFF_QK_WIDTH, DIFF_QK_WIDTH, DIFF_V_WIDTH,
             MLA_Q_RANK, MLA_KV_RANK, MLA_ROPE_DIM, N_BRANCH * D_MODEL)
    split_points = np.cumsum(sizes)[:-1].tolist()

    x = layer_norm(x, ln_in_g, ln_in_b)
    for l in range(DEPTH):
        lam_init = 0.8 - 0.6 * math.exp(-0.3 * l)
        proj = jnp.einsum('bsd,de->bse', x, w_in[l])
        (na_q, na_k, na_v, df_q, df_k, df_v, m_cq, m_ckv, m_kr, gate_pre) = jnp.split(proj, split_points, axis=-1)

        na_out = neighbourhood_attention(
            na_q.reshape(b, s, NA_HEADS, NA_HEAD_DIM), na_k.reshape(b, s, NA_HEADS, NA_HEAD_DIM),
            na_v.reshape(b, s, NA_HEADS, NA_HEAD_DIM), na_rpb[l])
        df_out = diff_attention(df_q, df_k, df_v, diff_lambda[l], diff_subln_g[l], lam_init)
        mla_out = latent_attention(m_cq, m_ckv, m_kr, w_mla_qb[l], w_mla_kvb[l],
                                   mla_q_norm_g[l], mla_kv_norm_g[l])

        gates = jax.nn.sigmoid((gate_pre + b_gate[l]).astype(jnp.float32)).astype(x.dtype)
        gates = gates.reshape(b, s, N_BRANCH, D_MODEL)
        merged = (gates[:, :, 0] * jnp.einsum('bse,ed->bsd', na_out, w_branch[l, 0])
                  + gates[:, :, 1] * jnp.einsum('bse,ed->bsd', df_out, w_branch[l, 1])
                  + gates[:, :, 2] * jnp.einsum('bse,ed->bsd', mla_out, w_branch[l, 2]))
        mix = jnp.einsum('bsd,de->bse', merged, w_out[l])
        x = layer_norm(DN_ALPHA * x + mix, ln1_g[l], ln1_b[l])

        gu = jnp.einsum('bsd,df->bsf', x, w_ffn_in[l])
        hidden = jax.nn.silu(gu[..., :D_FF]) * gu[..., D_FF:]
        ffn = jnp.einsum('bsf,fd->bsd', hidden, w_ffn_out[l])
        x = layer_norm(DN_ALPHA * x + ffn, ln2_g[l], ln2_b[l])
    return x
```

```python
import functools
import math

import numpy as np
import jax
import jax.numpy as jnp
from jax import lax
from jax.experimental import pallas as pl
from jax.experimental.pallas import tpu as pltpu

F32 = jnp.float32
BF16 = jnp.bfloat16

LANES = 128
VMEM_LIMIT_BYTES = 56 * 1024 * 1024

D_MODEL = 1024
GRID_W = 64
NA_HEADS = 8
NA_HEAD_DIM = 64
NA_WIN_ROWS = 8
NA_WIN_COLS = 16
DIFF_HEADS = 4
DIFF_HEAD_DIM = 64
DIFF_ROT_DIM = DIFF_HEAD_DIM // 4
MLA_HEADS = 8
MLA_NOPE_DIM = 64
MLA_ROPE_DIM = 32
MLA_V_DIM = 64
MLA_Q_RANK = 384
MLA_KV_RANK = 256
N_BRANCH = 3
BRANCH_WIDTH = 512
ROPE_THETA = 500000.0
LN_EPS = 1e-5
RMS_EPS = 1e-6
LOG2E = math.log2(math.e)

C_NAQ, C_NAK, C_NAV = 0, 512, 1024
C_DFQ, C_DFK, C_DFV = 1536, 2048, 2560
C_CQ = 3072
C_CKV = C_CQ + MLA_Q_RANK
C_KR = C_CKV + MLA_KV_RANK
C_END = C_KR + LANES
N_PROJ = 3 * 512 + 3 * 512 + MLA_Q_RANK + MLA_KV_RANK + MLA_ROPE_DIM

NA_RB = 8
NA_PAIR_Q = 2 * GRID_W
NA_BAND = 10 * GRID_W
NA_SLAB_ROWS = NA_RB + NA_WIN_ROWS
NA_TABLES = (((3, 0), (3, 1)),
             ((7, 0), (6, 0)),
             ((5, 0), (4, 0)),
             ((3, 2), (2, 2)),
             ((1, 2), (0, 2)))
NA_OFF_FIRST, NA_TAB_FIRST = (0, 0, 0, 2), (1, 2, 0, 0)
NA_OFF_LAST, NA_TAB_LAST = (4, 6, 6, 6), (0, 0, 3, 4)


def _cparams(*sem):
    return pltpu.CompilerParams(dimension_semantics=sem, vmem_limit_bytes=VMEM_LIMIT_BYTES)


def _dot(a, b):
    return jnp.dot(a, b, preferred_element_type=F32)


def _dot_nt(a, b):
    return lax.dot_general(a, b, (((1,), (1,)), ((), ())), preferred_element_type=F32)


def _layer_norm(x, g, b):
    mu = jnp.mean(x, axis=-1, keepdims=True)
    xc = x - mu
    var = jnp.mean(xc * xc, axis=-1, keepdims=True)
    return xc * lax.rsqrt(var + LN_EPS) * g + b


def _rms_norm(x, g):
    return x * lax.rsqrt(jnp.mean(x * x, axis=-1, keepdims=True) + RMS_EPS) * g


def _ln_kernel(x_ref, g_ref, b_ref, o_ref):
    o_ref[...] = _layer_norm(x_ref[...], g_ref[...], b_ref[...])


def _ln_call(x, g, b, tm=512):
    n, d = x.shape
    return pl.pallas_call(
        _ln_kernel,
        out_shape=jax.ShapeDtypeStruct((n, d), F32),
        grid=(n // tm,),
        in_specs=[pl.BlockSpec((tm, d), lambda i: (i, 0)),
                  pl.BlockSpec((1, d), lambda i: (0, 0)),
                  pl.BlockSpec((1, d), lambda i: (0, 0))],
        out_specs=pl.BlockSpec((tm, d), lambda i: (i, 0)),
        compiler_params=_cparams("parallel"),
        name="ln_in",
    )(x, g.reshape(1, d), b.reshape(1, d))


def _rope_lanes(x, c, sa, sb, shift):
    return x * c + pltpu.roll(x, LANES - shift, 1) * sa + pltpu.roll(x, shift, 1) * sb


def _proj_kernel(x_ref, w_ref, wq_ref, wk_ref, wv_ref, qg_ref, kvg_ref, vone_ref,
                 dc_ref, dsa_ref, dsb_ref, mc_ref, msa_ref, msb_ref,
                 naq_ref, nak_ref, nav_ref, dfq_ref, dfk_ref, dfv_ref,
                 mlq_ref, mlk_ref, mlv_ref, *, na_qscale, df_qscale, ml_qscale):
    xb = x_ref[...].astype(BF16)

    def mm(lo, hi):
        return _dot(xb, w_ref[:, lo:hi])

    n_hp = naq_ref.shape[0]
    y = mm(C_NAQ, C_NAQ + 512) * na_qscale
    for hp in range(n_hp):
        naq_ref[hp] = y[:, hp * LANES:(hp + 1) * LANES].astype(BF16)
    y = mm(C_NAK, C_NAK + 512)
    for hp in range(n_hp):
        nak_ref[hp] = y[:, hp * LANES:(hp + 1) * LANES].astype(BF16)
    y = mm(C_NAV, C_NAV + 512)
    for hp in range(n_hp):
        nav_ref[hp] = y[:, hp * LANES:(hp + 1) * LANES].astype(BF16)

    dc, dsa, dsb = dc_ref[...], dsa_ref[...], dsb_ref[...]
    half = DIFF_ROT_DIM // 2
    y = mm(C_DFQ, C_DFQ + 512)
    for h in range(DIFF_HEADS):
        blk = _rope_lanes(y[:, h * LANES:(h + 1) * LANES], dc, dsa, dsb, half)
        dfq_ref[:, h * LANES:(h + 1) * LANES] = (blk * df_qscale).astype(BF16)
    y = mm(C_DFK, C_DFK + 512)
    for h in range(DIFF_HEADS):
        blk = _rope_lanes(y[:, h * LANES:(h + 1) * LANES], dc, dsa, dsb, half)
        dfk_ref[:, h * LANES:(h + 1) * LANES] = blk.astype(BF16)
    dfv_ref[...] = mm(C_DFV, C_DFV + 512).astype(BF16)

    mc, msa, msb = mc_ref[...], msa_ref[...], msb_ref[...]
    mhalf = MLA_ROPE_DIM // 2
    cq = _rms_norm(mm(C_CQ, C_CKV), qg_ref[...]).astype(BF16)
    yq = _dot(cq, wq_ref[...])
    for h in range(MLA_HEADS):
        blk = _rope_lanes(yq[:, h * LANES:(h + 1) * LANES], mc, msa, msb, mhalf)
        mlq_ref[:, h * LANES:(h + 1) * LANES] = (blk * ml_qscale).astype(BF16)
    ckv = _rms_norm(mm(C_CKV, C_KR), kvg_ref[...]).astype(BF16)
    kr = mm(C_KR, C_END)
    yk = _dot(ckv, wk_ref[...])
    for h in range(MLA_HEADS):
        blk = _rope_lanes(yk[:, h * LANES:(h + 1) * LANES] + kr, mc, msa, msb, mhalf)
        mlk_ref[:, h * LANES:(h + 1) * LANES] = blk.astype(BF16)
    mlv_ref[...] = (_dot(ckv, wv_ref[...]) + vone_ref[...]).astype(BF16)


def _proj_call(x, w, wq, wk, wv, qg, kvg, vone, dtabs, mtabs, seq, tm=256):
    n, d = x.shape
    n_pos_blocks = seq // tm
    const = lambda i: (0, 0)
    row = lambda i: (i, 0)
    pos = lambda i: (i % n_pos_blocks, 0)
    hp3 = lambda i: (0, i, 0)
    n_hp = NA_HEADS * NA_HEAD_DIM // LANES
    kern = functools.partial(
        _proj_kernel,
        na_qscale=NA_HEAD_DIM ** -0.5 * LOG2E,
        df_qscale=DIFF_HEAD_DIM ** -0.5 * LOG2E,
        ml_qscale=(MLA_NOPE_DIM + MLA_ROPE_DIM) ** -0.5 * LOG2E)
    tab_spec = pl.BlockSpec((tm, LANES), pos)
    na_shape = jax.ShapeDtypeStruct((n_hp, n, LANES), BF16)
    return pl.pallas_call(
        kern,
        out_shape=(na_shape, na_shape, na_shape,
                   jax.ShapeDtypeStruct((n, 512), BF16),
                   jax.ShapeDtypeStruct((n, 512), BF16),
                   jax.ShapeDtypeStruct((n, 512), BF16),
                   jax.ShapeDtypeStruct((n, MLA_HEADS * LANES), BF16),
                   jax.ShapeDtypeStruct((n, MLA_HEADS * LANES), BF16),
                   jax.ShapeDtypeStruct((n, MLA_HEADS * LANES), BF16)),
        grid=(n // tm,),
        in_specs=[pl.BlockSpec((tm, d), row),
                  pl.BlockSpec(w.shape, const),
                  pl.BlockSpec(wq.shape, const),
                  pl.BlockSpec(wk.shape, const),
                  pl.BlockSpec(wv.shape, const),
                  pl.BlockSpec(qg.shape, const),
                  pl.BlockSpec(kvg.shape, const),
                  pl.BlockSpec(vone.shape, const),
                  tab_spec, tab_spec, tab_spec, tab_spec, tab_spec, tab_spec],
        out_specs=(pl.BlockSpec((n_hp, tm, LANES), hp3),
                   pl.BlockSpec((n_hp, tm, LANES), hp3),
                   pl.BlockSpec((n_hp, tm, LANES), hp3),
                   pl.BlockSpec((tm, 512), row),
                   pl.BlockSpec((tm, 512), row),
                   pl.BlockSpec((tm, 512), row),
                   pl.BlockSpec((tm, MLA_HEADS * LANES), row),
                   pl.BlockSpec((tm, MLA_HEADS * LANES), row),
                   pl.BlockSpec((tm, MLA_HEADS * LANES), row)),
        compiler_params=_cparams("parallel"),
        name="proj",
    )(x, w, wq, wk, wv, qg, kvg, vone, *dtabs, *mtabs)


def _attend(q, k_ref, k_lo, v_ref, v_lo, v_w, s_ref, tk):
    tq = q.shape[0]
    n_chunks = k_ref.shape[0] // tk
    mw = None
    for c in range(n_chunks):
        s = _dot_nt(q, k_ref[c * tk:(c + 1) * tk, k_lo:k_lo + LANES])
        s_ref[c] = s
        for j in range(tk // LANES):
            blk = s[:, j * LANES:(j + 1) * LANES]
            mw = blk if mw is None else jnp.maximum(mw, blk)
    m = jnp.broadcast_to(jnp.max(mw, axis=-1, keepdims=True), (tq, LANES))
    acc = None
    for c in range(n_chunks):
        s = s_ref[c]
        p = jnp.concatenate(
            [jnp.exp2(s[:, j * LANES:(j + 1) * LANES] - m) for j in range(tk // LANES)],
            axis=1).astype(BF16)
        o = _dot(p, v_ref[c * tk:(c + 1) * tk, v_lo:v_lo + v_w])
        acc = o if acc is None else acc + o
    return acc


def _diff_attn_kernel(q_ref, k_ref, v_ref, lam_ref, g_ref, o_ref, vx_ref, s1_ref, s2_ref,
                      *, lam_init, tk):
    @pl.when(pl.program_id(2) == 0)
    def _():
        vx_ref[:, :LANES] = v_ref[...]
        vx_ref[:, LANES:] = jnp.ones((v_ref.shape[0], LANES), BF16)

    q = q_ref[...]
    lane = lax.broadcasted_iota(jnp.int32, q.shape, 1)
    zero = jnp.zeros_like(q)
    q1 = jnp.where(lane < DIFF_HEAD_DIM, q, zero)
    q2 = jnp.where(lane >= DIFF_HEAD_DIM, q, zero)
    acc1 = _attend(q1, k_ref, 0, vx_ref, 0, 2 * LANES, s1_ref, tk)
    acc2 = _attend(q2, k_ref, 0, vx_ref, 0, 2 * LANES, s2_ref, tk)
    a1 = acc1[:, :LANES] / acc1[:, LANES:]
    a2 = acc2[:, :LANES] / acc2[:, LANES:]

    lv = lam_ref[...]
    d1 = jnp.sum(lv[0:1] * lv[1:2], axis=-1, keepdims=True)
    d2 = jnp.sum(lv[2:3] * lv[3:4], axis=-1, keepdims=True)
    lam = jnp.exp(d1) - jnp.exp(d2) + lam_init
    o = a1 - lam * a2
    o_ref[...] = (_rms_norm(o, g_ref[...]) * (1.0 - lam_init)).astype(o_ref.dtype)


def _diff_attn_call(q, k, v, lam_vecs, subln_g, lam_init, batch, seq, tq=512, tk=1024):
    n = q.shape[0]
    nq = seq // tq
    kern = functools.partial(_diff_attn_kernel, lam_init=lam_init, tk=tk)
    return pl.pallas_call(
        kern,
        out_shape=jax.ShapeDtypeStruct((n, DIFF_HEADS * LANES), BF16),
        grid=(batch, DIFF_HEADS, nq),
        in_specs=[pl.BlockSpec((tq, LANES), lambda b, h, i: (b * nq + i, h)),
                  pl.BlockSpec((seq, LANES), lambda b, h, i: (b, h)),
                  pl.BlockSpec((seq, LANES), lambda b, h, i: (b, h)),
                  pl.BlockSpec(lam_vecs.shape, lambda b, h, i: (0, 0)),
                  pl.BlockSpec((1, LANES), lambda b, h, i: (0, 0))],
        out_specs=pl.BlockSpec((tq, LANES), lambda b, h, i: (b * nq + i, h)),
        scratch_shapes=[pltpu.VMEM((seq, 2 * LANES), BF16),
                        pltpu.VMEM((seq // tk, tq, tk), F32),
                        pltpu.VMEM((seq // tk, tq, tk), F32)],
        compiler_params=_cparams("parallel", "parallel", "arbitrary"),
        name="diff_attn",
    )(q, k, v, lam_vecs, subln_g.reshape(1, LANES))


def _mla_attn_kernel(q_ref, k_ref, v_ref, o_ref, s1_ref, s2_ref, *, tk):
    acc_e = _attend(q_ref[:, :LANES], k_ref, 0, v_ref, 0, LANES, s1_ref, tk)
    acc_o = _attend(q_ref[:, LANES:], k_ref, LANES, v_ref, LANES, LANES, s2_ref, tk)
    lane = lax.broadcasted_iota(jnp.int32, acc_e.shape, 1)
    lo = lane < MLA_V_DIM
    num = jnp.where(lo, acc_e, acc_o)
    den = jnp.where(lo, pltpu.roll(acc_e, MLA_V_DIM, 1), pltpu.roll(acc_o, MLA_V_DIM, 1))
    o_ref[...] = (num / den).astype(o_ref.dtype)


def _mla_attn_call(q, k, v, batch, seq, tq=512, tk=1024):
    n = q.shape[0]
    nq = seq // tq
    n_pairs = MLA_HEADS // 2
    kern = functools.partial(_mla_attn_kernel, tk=tk)
    return pl.pallas_call(
        kern,
        out_shape=jax.ShapeDtypeStruct((n, n_pairs * LANES), BF16),
        grid=(batch, n_pairs, nq),
        in_specs=[pl.BlockSpec((tq, 2 * LANES), lambda b, h, i: (b * nq + i, h)),
                  pl.BlockSpec((seq, 2 * LANES), lambda b, h, i: (b, h)),
                  pl.BlockSpec((seq, 2 * LANES), lambda b, h, i: (b, h))],
        out_specs=pl.BlockSpec((tq, LANES), lambda b, h, i: (b * nq + i, h)),
        scratch_shapes=[pltpu.VMEM((seq // tk, tq, tk), F32),
                        pltpu.VMEM((seq // tk, tq, tk), F32)],
        compiler_params=_cparams("parallel", "parallel", "arbitrary"),
        name="mla_attn",
    )(q, k, v)


def _na_bias_index():
    n_dc = 2 * NA_WIN_COLS - 1
    idx = np.zeros((len(NA_TABLES), NA_PAIR_Q, NA_BAND), np.int32)
    valid = np.zeros((len(NA_TABLES), NA_PAIR_Q, NA_BAND), bool)
    c = np.arange(GRID_W)[:, None]
    cp = np.arange(GRID_W)[None, :]
    wstart = np.clip(c - NA_WIN_COLS // 2, 0, GRID_W - NA_WIN_COLS)
    col_ok = (cp >= wstart) & (cp < wstart + NA_WIN_COLS)
    dcol = np.clip(cp - c, -(NA_WIN_COLS - 1), NA_WIN_COLS - 1) + (NA_WIN_COLS - 1)
    for t, rows in enumerate(NA_TABLES):
        for r, (dr0, pos) in enumerate(rows):
            for jb in range(NA_BAND // GRID_W):
                a = jb - pos
                if 0 <= a < NA_WIN_ROWS:
                    qs = slice(r * GRID_W, (r + 1) * GRID_W)
                    ks = slice(jb * GRID_W, (jb + 1) * GRID_W)
                    idx[t, qs, ks] = (dr0 + a) * n_dc + dcol
                    valid[t, qs, ks] = col_ok
    return idx, valid


def _na_kernel(q_ref, k_ref, v_ref, bias_ref, o_ref):
    rb = pl.program_id(2)
    n_rb = pl.num_programs(2)
    start = jnp.clip(NA_RB * rb - NA_WIN_ROWS // 2, 0, GRID_W - NA_SLAB_ROWS)
    first = rb == 0
    last = rb == n_rb - 1
    lane = lax.broadcasted_iota(jnp.int32, (NA_PAIR_Q, LANES), 1)
    lo = lane < NA_HEAD_DIM
    for p in range(NA_RB // 2):
        off = jnp.where(first, NA_OFF_FIRST[p], jnp.where(last, NA_OFF_LAST[p], 2 * p))
        tab = jnp.where(first, NA_TAB_FIRST[p], jnp.where(last, NA_TAB_LAST[p], 0))
        tok0 = pl.multiple_of((start + off) * GRID_W, GRID_W)
        kb = k_ref[0, pl.ds(tok0, NA_BAND), :]
        vb = v_ref[0, pl.ds(tok0, NA_BAND), :]
        qp = q_ref[0, p * NA_PAIR_Q:(p + 1) * NA_PAIR_Q, :]
        zero = jnp.zeros_like(qp)
        outs = []
        for hh in range(2):
            qm = jnp.where(lo if hh == 0 else jnp.logical_not(lo), qp, zero)
            s = _dot_nt(qm, kb) + bias_ref[0, hh, tab]
            m = jnp.max(s, axis=-1, keepdims=True)
            e = jnp.exp2(s - m)
            l = jnp.sum(e, axis=-1, keepdims=True)
            outs.append(_dot(e.astype(BF16), vb) / l)
        o_ref[0, p * NA_PAIR_Q:(p + 1) * NA_PAIR_Q, :] = jnp.where(lo, outs[0], outs[1]).astype(o_ref.dtype)


def _na_call(q, k, v, bias, batch, seq):
    n_hp, n, _ = q.shape
    n_rb = seq // (NA_RB * GRID_W)
    blk_q = NA_RB * GRID_W
    return pl.pallas_call(
        _na_kernel,
        out_shape=jax.ShapeDtypeStruct((n_hp, n, LANES), BF16),
        grid=(n_hp, batch, n_rb),
        in_specs=[pl.BlockSpec((1, blk_q, LANES), lambda hp, b, r: (hp, b * n_rb + r, 0)),
                  pl.BlockSpec((1, seq, LANES), lambda hp, b, r: (hp, b, 0)),
                  pl.BlockSpec((1, seq, LANES), lambda hp, b, r: (hp, b, 0)),
                  pl.BlockSpec((1,) + bias.shape[1:], lambda hp, b, r: (hp, 0, 0, 0, 0))],
        out_specs=pl.BlockSpec((1, blk_q, LANES), lambda hp, b, r: (hp, b * n_rb + r, 0)),
        compiler_params=_cparams("parallel", "parallel", "arbitrary"),
        name="na_attn",
    )(q, k, v, bias)


def _merge_kernel(x_ref, na_ref, df_ref, ml_ref, wg_ref, bg_ref, wb_ref, wo_ref, g_ref, b_ref,
                  o_ref, *, alpha):
    x = x_ref[...]
    xb = x.astype(BF16)
    na = jnp.concatenate([na_ref[hp] for hp in range(na_ref.shape[0])], axis=1)
    branches = (na, df_ref[...], ml_ref[...])
    merged = None
    for br in range(N_BRANCH):
        z = _dot(xb, wg_ref[:, br * D_MODEL:(br + 1) * D_MODEL]) + bg_ref[:, br * D_MODEL:(br + 1) * D_MODEL]
        t = jax.nn.sigmoid(z) * _dot(branches[br], wb_ref[br])
        merged = t if merged is None else merged + t
    mix = _dot(merged.astype(BF16), wo_ref[...])
    o_ref[...] = _layer_norm(alpha * x + mix, g_ref[...], b_ref[...])


def _merge_call(x, na, df, ml, wg, bg, wb, wo, g, b, alpha, tm=256):
    n, d = x.shape
    n_hp = na.shape[0]
    const2 = lambda i: (0, 0)
    row = lambda i: (i, 0)
    return pl.pallas_call(
        functools.partial(_merge_kernel, alpha=alpha),
        out_shape=jax.ShapeDtypeStruct((n, d), F32),
        grid=(n // tm,),
        in_specs=[pl.BlockSpec((tm, d), row),
                  pl.BlockSpec((n_hp, tm, LANES), lambda i: (0, i, 0)),
                  pl.BlockSpec((tm, BRANCH_WIDTH), row),
                  pl.BlockSpec((tm, BRANCH_WIDTH), row),
                  pl.BlockSpec(wg.shape, const2),
                  pl.BlockSpec((1, N_BRANCH * d), const2),
                  pl.BlockSpec(wb.shape, lambda i: (0, 0, 0)),
                  pl.BlockSpec(wo.shape, const2),
                  pl.BlockSpec((1, d), const2),
                  pl.BlockSpec((1, d), const2)],
        out_specs=pl.BlockSpec((tm, d), row),
        compiler_params=_cparams("parallel"),
        name="merge_ln",
    )(x, na, df, ml, wg, bg.reshape(1, -1), wb, wo, g.reshape(1, d), b.reshape(1, d))


def _ffn_kernel(x_ref, w1_ref, w2_ref, g_ref, b_ref, o_ref, *, alpha, d_ff, n_chunks):
    x = x_ref[...]
    xb = x.astype(BF16)
    fc = d_ff // n_chunks
    acc = None
    for c in range(n_chunks):
        gate = _dot(xb, w1_ref[:, c * fc:(c + 1) * fc])
        up = _dot(xb, w1_ref[:, d_ff + c * fc:d_ff + (c + 1) * fc])
        hid = (gate * jax.nn.sigmoid(gate) * up).astype(BF16)
        o = _dot(hid, w2_ref[c * fc:(c + 1) * fc, :])
        acc = o if acc is None else acc + o
    o_ref[...] = _layer_norm(alpha * x + acc, g_ref[...], b_ref[...])


def _ffn_call(x, w1, w2, g, b, alpha, tm=256, n_chunks=2):
    n, d = x.shape
    d_ff = w2.shape[0]
    const2 = lambda i: (0, 0)
    row = lambda i: (i, 0)
    return pl.pallas_call(
        functools.partial(_ffn_kernel, alpha=alpha, d_ff=d_ff, n_chunks=n_chunks),
        out_shape=jax.ShapeDtypeStruct((n, d), F32),
        grid=(n // tm,),
        in_specs=[pl.BlockSpec((tm, d), row),
                  pl.BlockSpec(w1.shape, const2),
                  pl.BlockSpec(w2.shape, const2),
                  pl.BlockSpec((1, d), const2),
                  pl.BlockSpec((1, d), const2)],
        out_specs=pl.BlockSpec((tm, d), row),
        compiler_params=_cparams("parallel"),
        name="ffn_ln",
    )(x, w1, w2, g.reshape(1, d), b.reshape(1, d))


def _rope_tables(seq, rot_dim, lane_period, lane_lo):
    half = rot_dim // 2
    inv_freq = jnp.exp(-math.log(ROPE_THETA) * jnp.arange(half, dtype=F32) / half)
    ang = jnp.arange(seq, dtype=F32)[:, None] * inv_freq[None, :]
    cos, sin = jnp.cos(ang), jnp.sin(ang)
    d = (np.arange(LANES) % lane_period) - lane_lo
    is1 = (d >= 0) & (d < half)
    is2 = (d >= half) & (d < rot_dim)
    fidx = np.where(is1, d, np.where(is2, d - half, 0))
    cos_l, sin_l = cos[:, fidx], sin[:, fidx]
    c = jnp.where(is1 | is2, cos_l, 1.0)
    sa = jnp.where(is1, -sin_l, 0.0)
    sb = jnp.where(is2, sin_l, 0.0)
    return c, sa, sb


def _pad_cols(w, n_out, groups):
    out = jnp.zeros((w.shape[0], n_out), w.dtype)
    for dst, src, wd in groups:
        out = out.at[:, dst:dst + wd].set(w[:, src:src + wd])
    return out


def _layer_params(l, w_in, w_mla_qb, w_mla_kvb):
    d = w_in.shape[1]
    w = jnp.zeros((d, C_END), F32)
    w = w.at[:, :C_KR].set(w_in[l, :, :C_KR])
    w = w.at[:, C_KR + MLA_NOPE_DIM:C_KR + MLA_NOPE_DIM + MLA_ROPE_DIM].set(w_in[l, :, C_KR:N_PROJ])
    wg = w_in[l, :, N_PROJ:]
    qd = MLA_NOPE_DIM + MLA_ROPE_DIM
    n_pad = MLA_HEADS * LANES
    wq = _pad_cols(w_mla_qb[l], n_pad, [(h * LANES, h * qd, qd) for h in range(MLA_HEADS)])
    kvd = MLA_NOPE_DIM + MLA_V_DIM
    wk = _pad_cols(w_mla_kvb[l], n_pad, [(h * LANES, h * kvd, MLA_NOPE_DIM) for h in range(MLA_HEADS)])
    wv = _pad_cols(w_mla_kvb[l], n_pad,
                   [(h * LANES + (h % 2) * MLA_V_DIM, h * kvd + MLA_NOPE_DIM, MLA_V_DIM)
                    for h in range(MLA_HEADS)])
    return w.astype(BF16), wg.astype(BF16), wq.astype(BF16), wk.astype(BF16), wv.astype(BF16)


def _v_ones_row():
    lane = np.arange(MLA_HEADS * LANES)
    h, r = lane // LANES, lane % LANES
    ones = np.where(h % 2 == 0, r >= MLA_V_DIM, r < MLA_V_DIM)
    return jnp.asarray(ones.astype(np.float32)).reshape(1, -1)


def _na_bias_tables(rpb_l, idx, valid):
    h = rpb_l.shape[0]
    flat = rpb_l.reshape(h, -1) * LOG2E
    tab = jnp.take(flat, jnp.asarray(idx.reshape(-1)), axis=1).reshape((h,) + idx.shape)
    tab = jnp.where(jnp.asarray(valid)[None], tab, -jnp.inf)
    return tab.reshape((h // 2, 2) + idx.shape)


def kernel(x, ln_in_g, ln_in_b, w_in, b_gate, na_rpb, diff_lambda, diff_subln_g, mla_q_norm_g,
           mla_kv_norm_g, w_mla_qb, w_mla_kvb, w_branch, w_out, ln1_g, ln1_b, w_ffn_in, w_ffn_out,
           ln2_g, ln2_b):
    batch, seq, d = x.shape
    depth = w_in.shape[0]
    assert d == D_MODEL and seq == GRID_W * GRID_W and w_in.shape[2] == N_PROJ + N_BRANCH * D_MODEL
    alpha = (2 * depth) ** 0.25

    dtabs = _rope_tables(seq, DIFF_ROT_DIM, DIFF_HEAD_DIM, 0)
    mtabs = _rope_tables(seq, MLA_ROPE_DIM, LANES, MLA_NOPE_DIM)
    vone = _v_ones_row()
    na_idx, na_valid = _na_bias_index()

    h = _ln_call(x.reshape(batch * seq, d), ln_in_g, ln_in_b)
    for l in range(depth):
        lam_init = 0.8 - 0.6 * math.exp(-0.3 * l)
        w, wg, wq, wk, wv = _layer_params(l, w_in, w_mla_qb, w_mla_kvb)
        (naq, nak, nav, dfq, dfk, dfv, mlq, mlk, mlv) = _proj_call(
            h, w, wq, wk, wv, mla_q_norm_g[l].reshape(1, -1), mla_kv_norm_g[l].reshape(1, -1),
            vone, dtabs, mtabs, seq)
        na_out = _na_call(naq, nak, nav, _na_bias_tables(na_rpb[l], na_idx, na_valid), batch, seq)
        df_out = _diff_attn_call(dfq, dfk, dfv, diff_lambda[l], diff_subln_g[l], lam_init, batch, seq)
        ml_out = _mla_attn_call(mlq, mlk, mlv, batch, seq)
        h = _merge_call(h, na_out, df_out, ml_out, wg, b_gate[l], w_branch[l].astype(BF16),
                        w_out[l].astype(BF16), ln1_g[l], ln1_b[l], alpha)
        h = _ffn_call(h, w_ffn_in[l].astype(BF16), w_ffn_out[l].astype(BF16), ln2_g[l], ln2_b[l], alpha)
    return h.reshape(batch, seq, d)
```

```python
import functools
import math

import numpy as np
import jax
import jax.numpy as jnp
from jax import lax
from jax.experimental import pallas as pl
from jax.experimental.pallas import tpu as pltpu

F32 = jnp.float32
BF16 = jnp.bfloat16

LANES = 128
VMEM_LIMIT_BYTES = 56 * 1024 * 1024

D_MODEL = 1024
GRID_W = 64
NA_HEADS = 8
NA_HEAD_DIM = 64
NA_WIN_ROWS = 8
NA_WIN_COLS = 16
DIFF_HEADS = 4
DIFF_HEAD_DIM = 64
DIFF_ROT_DIM = DIFF_HEAD_DIM // 4
MLA_HEADS = 8
MLA_NOPE_DIM = 64
MLA_ROPE_DIM = 32
MLA_V_DIM = 64
MLA_Q_RANK = 384
MLA_KV_RANK = 256
N_BRANCH = 3
BRANCH_WIDTH = 512
ROPE_THETA = 500000.0
LN_EPS = 1e-5
RMS_EPS = 1e-6
LOG2E = math.log2(math.e)

C_NAQ, C_NAK, C_NAV = 0, 512, 1024
C_DFQ, C_DFK, C_DFV = 1536, 2048, 2560
C_CQ = 3072
C_CKV = C_CQ + MLA_Q_RANK
C_KR = C_CKV + MLA_KV_RANK
C_END = C_KR + LANES
N_PROJ = 3 * 512 + 3 * 512 + MLA_Q_RANK + MLA_KV_RANK + MLA_ROPE_DIM

NA_RB = 8
NA_PAIR_Q = 2 * GRID_W
NA_BAND = 10 * GRID_W
NA_SLAB_ROWS = NA_RB + NA_WIN_ROWS
NA_TABLES = (((3, 0), (3, 1)),
             ((7, 0), (6, 0)),
             ((5, 0), (4, 0)),
             ((3, 2), (2, 2)),
             ((1, 2), (0, 2)))
NA_OFF_FIRST, NA_TAB_FIRST = (0, 0, 0, 2), (1, 2, 0, 0)
NA_OFF_LAST, NA_TAB_LAST = (4, 6, 6, 6), (0, 0, 3, 4)


def _cparams(*sem):
    return pltpu.CompilerParams(dimension_semantics=sem, vmem_limit_bytes=VMEM_LIMIT_BYTES)


def _dot(a, b):
    return jnp.dot(a, b, preferred_element_type=F32)


def _dot_nt(a, b):
    return lax.dot_general(a, b, (((1,), (1,)), ((), ())), preferred_element_type=F32)


def _layer_norm(x, g, b):
    mu = jnp.mean(x, axis=-1, keepdims=True)
    xc = x - mu
    var = jnp.mean(xc * xc, axis=-1, keepdims=True)
    return xc * lax.rsqrt(var + LN_EPS) * g + b


def _rms_norm(x, g):
    return x * lax.rsqrt(jnp.mean(x * x, axis=-1, keepdims=True) + RMS_EPS) * g


def _ln_kernel(x_ref, g_ref, b_ref, o_ref):
    o_ref[...] = _layer_norm(x_ref[...], g_ref[...], b_ref[...])


def _ln_call(x, g, b, tm=512):
    n, d = x.shape
    return pl.pallas_call(
        _ln_kernel,
        out_shape=jax.ShapeDtypeStruct((n, d), F32),
        grid=(n // tm,),
        in_specs=[pl.BlockSpec((tm, d), lambda i: (i, 0)),
                  pl.BlockSpec((1, d), lambda i: (0, 0)),
                  pl.BlockSpec((1, d), lambda i: (0, 0))],
        out_specs=pl.BlockSpec((tm, d), lambda i: (i, 0)),
        compiler_params=_cparams("parallel"),
        name="ln_in",
    )(x, g.reshape(1, d), b.reshape(1, d))


def _rope_lanes(x, c, sa, sb, shift):
    return x * c + pltpu.roll(x, LANES - shift, 1) * sa + pltpu.roll(x, shift, 1) * sb


def _proj_kernel(x_ref, w_ref, wq_ref, wk_ref, wv_ref, qg_ref, kvg_ref, vone_ref,
                 dc_ref, dsa_ref, dsb_ref, mc_ref, msa_ref, msb_ref,
                 naq_ref, nak_ref, nav_ref, dfq_ref, dfk_ref, dfv_ref,
                 mlq_ref, mlk_ref, mlv_ref, *, na_qscale, df_qscale, ml_qscale):
    xb = x_ref[...].astype(BF16)

    def mm(lo, hi):
        return _dot(xb, w_ref[:, lo:hi])

    n_hp = naq_ref.shape[0]
    y = mm(C_NAQ, C_NAQ + 512) * na_qscale
    for hp in range(n_hp):
        naq_ref[hp] = y[:, hp * LANES:(hp + 1) * LANES].astype(BF16)
    y = mm(C_NAK, C_NAK + 512)
    for hp in range(n_hp):
        nak_ref[hp] = y[:, hp * LANES:(hp + 1) * LANES].astype(BF16)
    y = mm(C_NAV, C_NAV + 512)
    for hp in range(n_hp):
        nav_ref[hp] = y[:, hp * LANES:(hp + 1) * LANES].astype(BF16)

    dc, dsa, dsb = dc_ref[...], dsa_ref[...], dsb_ref[...]
    half = DIFF_ROT_DIM // 2
    y = mm(C_DFQ, C_DFQ + 512)
    for h in range(DIFF_HEADS):
        blk = _rope_lanes(y[:, h * LANES:(h + 1) * LANES], dc, dsa, dsb, half)
        dfq_ref[:, h * LANES:(h + 1) * LANES] = (blk * df_qscale).astype(BF16)
    y = mm(C_DFK, C_DFK + 512)
    for h in range(DIFF_HEADS):
        blk = _rope_lanes(y[:, h * LANES:(h + 1) * LANES], dc, dsa, dsb, half)
        dfk_ref[:, h * LANES:(h + 1) * LANES] = blk.astype(BF16)
    dfv_ref[...] = mm(C_DFV, C_DFV + 512).astype(BF16)

    mc, msa, msb = mc_ref[...], msa_ref[...], msb_ref[...]
    mhalf = MLA_ROPE_DIM // 2
    cq = _rms_norm(mm(C_CQ, C_CKV), qg_ref[...]).astype(BF16)
    yq = _dot(cq, wq_ref[...])
    for h in range(MLA_HEADS):
        blk = _rope_lanes(yq[:, h * LANES:(h + 1) * LANES], mc, msa, msb, mhalf)
        mlq_ref[:, h * LANES:(h + 1) * LANES] = (blk * ml_qscale).astype(BF16)
    ckv = _rms_norm(mm(C_CKV, C_KR), kvg_ref[...]).astype(BF16)
    kr = mm(C_KR, C_END)
    yk = _dot(ckv, wk_ref[...])
    for h in range(MLA_HEADS):
        blk = _rope_lanes(yk[:, h * LANES:(h + 1) * LANES] + kr, mc, msa, msb, mhalf)
        mlk_ref[:, h * LANES:(h + 1) * LANES] = blk.astype(BF16)
    mlv_ref[...] = (_dot(ckv, wv_ref[...]) + vone_ref[...]).astype(BF16)


def _proj_call(x, w, wq, wk, wv, qg, kvg, vone, dtabs, mtabs, seq, tm=256):
    n, d = x.shape
    n_pos_blocks = seq // tm
    const = lambda i: (0, 0)
    row = lambda i: (i, 0)
    pos = lambda i: (i % n_pos_blocks, 0)
    hp3 = lambda i: (0, i, 0)
    n_hp = NA_HEADS * NA_HEAD_DIM // LANES
    kern = functools.partial(
        _proj_kernel,
        na_qscale=NA_HEAD_DIM ** -0.5 * LOG2E,
        df_qscale=DIFF_HEAD_DIM ** -0.5 * LOG2E,
        ml_qscale=(MLA_NOPE_DIM + MLA_ROPE_DIM) ** -0.5 * LOG2E)
    tab_spec = pl.BlockSpec((tm, LANES), pos)
    na_shape = jax.ShapeDtypeStruct((n_hp, n, LANES), BF16)
    return pl.pallas_call(
        kern,
        out_shape=(na_shape, na_shape, na_shape,
                   jax.ShapeDtypeStruct((n, 512), BF16),
                   jax.ShapeDtypeStruct((n, 512), BF16),
                   jax.ShapeDtypeStruct((n, 512), BF16),
                   jax.ShapeDtypeStruct((n, MLA_HEADS * LANES), BF16),
                   jax.ShapeDtypeStruct((n, MLA_HEADS * LANES), BF16),
                   jax.ShapeDtypeStruct((n, MLA_HEADS * LANES), BF16)),
        grid=(n // tm,),
        in_specs=[pl.BlockSpec((tm, d), row),
                  pl.BlockSpec(w.shape, const),
                  pl.BlockSpec(wq.shape, const),
                  pl.BlockSpec(wk.shape, const),
                  pl.BlockSpec(wv.shape, const),
                  pl.BlockSpec(qg.shape, const),
                  pl.BlockSpec(kvg.shape, const),
                  pl.BlockSpec(vone.shape, const),
                  tab_spec, tab_spec, tab_spec, tab_spec, tab_spec, tab_spec],
        out_specs=(pl.BlockSpec((n_hp, tm, LANES), hp3),
                   pl.BlockSpec((n_hp, tm, LANES), hp3),
                   pl.BlockSpec((n_hp, tm, LANES), hp3),
                   pl.BlockSpec((tm, 512), row),
                   pl.BlockSpec((tm, 512), row),
                   pl.BlockSpec((tm, 512), row),
                   pl.BlockSpec((tm, MLA_HEADS * LANES), row),
                   pl.BlockSpec((tm, MLA_HEADS * LANES), row),
                   pl.BlockSpec((tm, MLA_HEADS * LANES), row)),
        compiler_params=_cparams("parallel"),
        name="proj",
    )(x, w, wq, wk, wv, qg, kvg, vone, *dtabs, *mtabs)


def _attend(q, k_ref, k_lo, v_ref, v_lo, v_w, s_ref, tk):
    tq = q.shape[0]
    n_chunks = k_ref.shape[0] // tk
    mw = None
    for c in range(n_chunks):
        s = _dot_nt(q, k_ref[c * tk:(c + 1) * tk, k_lo:k_lo + LANES])
        s_ref[c] = s
        for j in range(tk // LANES):
            blk = s[:, j * LANES:(j + 1) * LANES]
            mw = blk if mw is None else jnp.maximum(mw, blk)
    m = jnp.broadcast_to(jnp.max(mw, axis=-1, keepdims=True), (tq, LANES))
    acc = None
    for c in range(n_chunks):
        s = s_ref[c]
        p = jnp.concatenate(
            [jnp.exp2(s[:, j * LANES:(j + 1) * LANES] - m) for j in range(tk // LANES)],
            axis=1).astype(BF16)
        o = _dot(p, v_ref[c * tk:(c + 1) * tk, v_lo:v_lo + v_w])
        acc = o if acc is None else acc + o
    return acc


def _diff_attn_kernel(q_ref, k_ref, v_ref, lam_ref, g_ref, o_ref, vx_ref, s1_ref, s2_ref,
                      *, lam_init, tk):
    @pl.when(pl.program_id(2) == 0)
    def _():
        vx_ref[:, :LANES] = v_ref[...]
        vx_ref[:, LANES:] = jnp.ones((v_ref.shape[0], LANES), BF16)

    q = q_ref[...]
    lane = lax.broadcasted_iota(jnp.int32, q.shape, 1)
    zero = jnp.zeros_like(q)
    q1 = jnp.where(lane < DIFF_HEAD_DIM, q, zero)
    q2 = jnp.where(lane >= DIFF_HEAD_DIM, q, zero)
    acc1 = _attend(q1, k_ref, 0, vx_ref, 0, 2 * LANES, s1_ref, tk)
    acc2 = _attend(q2, k_ref, 0, vx_ref, 0, 2 * LANES, s2_ref, tk)
    a1 = acc1[:, :LANES] / acc1[:, LANES:]
    a2 = acc2[:, :LANES] / acc2[:, LANES:]

    lv = lam_ref[...]
    d1 = jnp.sum(lv[0:1] * lv[1:2], axis=-1, keepdims=True)
    d2 = jnp.sum(lv[2:3] * lv[3:4], axis=-1, keepdims=True)
    lam = jnp.exp(d1) - jnp.exp(d2) + lam_init
    o = a1 - lam * a2
    o_ref[...] = (_rms_norm(o, g_ref[...]) * (1.0 - lam_init)).astype(o_ref.dtype)


def _diff_attn_call(q, k, v, lam_vecs, subln_g, lam_init, batch, seq, tq=512, tk=1024):
    n = q.shape[0]
    nq = seq // tq
    kern = functools.partial(_diff_attn_kernel, lam_init=lam_init, tk=tk)
    return pl.pallas_call(
        kern,
        out_shape=jax.ShapeDtypeStruct((n, DIFF_HEADS * LANES), BF16),
        grid=(batch, DIFF_HEADS, nq),
        in_specs=[pl.BlockSpec((tq, LANES), lambda b, h, i: (b * nq + i, h)),
                  pl.BlockSpec((seq, LANES), lambda b, h, i: (b, h)),
                  pl.BlockSpec((seq, LANES), lambda b, h, i: (b, h)),
                  pl.BlockSpec(lam_vecs.shape, lambda b, h, i: (0, 0)),
                  pl.BlockSpec((1, LANES), lambda b, h, i: (0, 0))],
        out_specs=pl.BlockSpec((tq, LANES), lambda b, h, i: (b * nq + i, h)),
        scratch_shapes=[pltpu.VMEM((seq, 2 * LANES), BF16),
                        pltpu.VMEM((seq // tk, tq, tk), F32),
                        pltpu.VMEM((seq // tk, tq, tk), F32)],
        compiler_params=_cparams("parallel", "parallel", "arbitrary"),
        name="diff_attn",
    )(q, k, v, lam_vecs, subln_g.reshape(1, LANES))


def _mla_attn_kernel(q_ref, k_ref, v_ref, o_ref, s1_ref, s2_ref, *, tk):
    acc_e = _attend(q_ref[:, :LANES], k_ref, 0, v_ref, 0, LANES, s1_ref, tk)
    acc_o = _attend(q_ref[:, LANES:], k_ref, LANES, v_ref, LANES, LANES, s2_ref, tk)
    lane = lax.broadcasted_iota(jnp.int32, acc_e.shape, 1)
    lo = lane < MLA_V_DIM
    num = jnp.where(lo, acc_e, acc_o)
    den = jnp.where(lo, pltpu.roll(acc_e, MLA_V_DIM, 1), pltpu.roll(acc_o, MLA_V_DIM, 1))
    o_ref[...] = (num / den).astype(o_ref.dtype)


def _mla_attn_call(q, k, v, batch, seq, tq=512, tk=1024):
    n = q.shape[0]
    nq = seq // tq
    n_pairs = MLA_HEADS // 2
    kern = functools.partial(_mla_attn_kernel, tk=tk)
    return pl.pallas_call(
        kern,
        out_shape=jax.ShapeDtypeStruct((n, n_pairs * LANES), BF16),
        grid=(batch, n_pairs, nq),
        in_specs=[pl.BlockSpec((tq, 2 * LANES), lambda b, h, i: (b * nq + i, h)),
                  pl.BlockSpec((seq, 2 * LANES), lambda b, h, i: (b, h)),
                  pl.BlockSpec((seq, 2 * LANES), lambda b, h, i: (b, h))],
        out_specs=pl.BlockSpec((tq, LANES), lambda b, h, i: (b * nq + i, h)),
        scratch_shapes=[pltpu.VMEM((seq // tk, tq, tk), F32),
                        pltpu.VMEM((seq // tk, tq, tk), F32)],
        compiler_params=_cparams("parallel", "parallel", "arbitrary"),
        name="mla_attn",
    )(q, k, v)


def _na_window_mask():
    neg = np.full((len(NA_TABLES), NA_PAIR_Q, NA_BAND), -np.inf, np.float32)
    c = np.arange(GRID_W)[:, None]
    cp = np.arange(GRID_W)[None, :]
    wstart = np.clip(c - NA_WIN_COLS // 2, 0, GRID_W - NA_WIN_COLS)
    col_ok = (cp >= wstart) & (cp < wstart + NA_WIN_COLS)
    for t, rows in enumerate(NA_TABLES):
        for r, (_, pos) in enumerate(rows):
            for jb in range(NA_BAND // GRID_W):
                if 0 <= jb - pos < NA_WIN_ROWS:
                    neg[t, r * GRID_W:(r + 1) * GRID_W, jb * GRID_W:(jb + 1) * GRID_W] = np.where(col_ok, 0.0, -np.inf)
    return neg


def _na_bias_kernel(p_ref, neg_ref, o_ref):
    for t, rows in enumerate(NA_TABLES):
        for r, (dr0, pos) in enumerate(rows):
            for u in range(NA_BAND // LANES):
                dr = max(dr0 - pos + 2 * u, 0)
                src = p_ref[0, dr % 2, dr // 2:dr // 2 + 1, :]
                tile = pltpu.roll(jnp.broadcast_to(src, (GRID_W, LANES)),
                                  LANES - (NA_WIN_COLS - 1), 1, stride=1, stride_axis=0)
                qs = slice(r * GRID_W, (r + 1) * GRID_W)
                ks = slice(u * LANES, (u + 1) * LANES)
                o_ref[0, t, qs, ks] = tile + neg_ref[t, qs, ks]


def _na_bias_tables(rpb_l, neg):
    h, n_dr, n_dc = rpb_l.shape
    n_rows = 2 * (n_dr // 2 + 2)
    t_pad = jnp.zeros((h, n_rows, GRID_W), F32).at[:, :n_dr, :n_dc].set(rpb_l * LOG2E)
    p_even = t_pad.reshape(h, n_rows // 2, LANES)
    p_odd = jnp.concatenate([t_pad[:, 1:], jnp.zeros((h, 1, GRID_W), F32)], axis=1).reshape(h, n_rows // 2, LANES)
    pairs = jnp.stack([p_even, p_odd], axis=1)
    out = pl.pallas_call(
        _na_bias_kernel,
        out_shape=jax.ShapeDtypeStruct((h,) + neg.shape, F32),
        grid=(h,),
        in_specs=[pl.BlockSpec((1,) + pairs.shape[1:], lambda i: (i, 0, 0, 0)),
                  pl.BlockSpec(neg.shape, lambda i: (0, 0, 0))],
        out_specs=pl.BlockSpec((1,) + neg.shape, lambda i: (i, 0, 0, 0)),
        compiler_params=_cparams("parallel"),
        name="na_bias",
    )(pairs, neg)
    return out.reshape((h // 2, 2) + neg.shape)


def _na_kernel(q_ref, k_ref, v_ref, bias_ref, o_ref):
    rb = pl.program_id(2)
    n_rb = pl.num_programs(2)
    start = jnp.clip(NA_RB * rb - NA_WIN_ROWS // 2, 0, GRID_W - NA_SLAB_ROWS)
    first = rb == 0
    last = rb == n_rb - 1
    lane = lax.broadcasted_iota(jnp.int32, (NA_PAIR_Q, LANES), 1)
    lo = lane < NA_HEAD_DIM
    for p in range(NA_RB // 2):
        off = jnp.where(first, NA_OFF_FIRST[p], jnp.where(last, NA_OFF_LAST[p], 2 * p))
        tab = jnp.where(first, NA_TAB_FIRST[p], jnp.where(last, NA_TAB_LAST[p], 0))
        tok0 = pl.multiple_of((start + off) * GRID_W, GRID_W)
        kb = k_ref[0, pl.ds(tok0, NA_BAND), :]
        vb = v_ref[0, pl.ds(tok0, NA_BAND), :]
        qp = q_ref[0, p * NA_PAIR_Q:(p + 1) * NA_PAIR_Q, :]
        zero = jnp.zeros_like(qp)
        outs = []
        for hh in range(2):
            qm = jnp.where(lo if hh == 0 else jnp.logical_not(lo), qp, zero)
            s = _dot_nt(qm, kb) + bias_ref[0, hh, tab]
            m = jnp.max(s, axis=-1, keepdims=True)
            e = jnp.exp2(s - m)
            l = jnp.sum(e, axis=-1, keepdims=True)
            outs.append(_dot(e.astype(BF16), vb) / l)
        o_ref[0, p * NA_PAIR_Q:(p + 1) * NA_PAIR_Q, :] = jnp.where(lo, outs[0], outs[1]).astype(o_ref.dtype)


def _na_call(q, k, v, bias, batch, seq):
    n_hp, n, _ = q.shape
    n_rb = seq // (NA_RB * GRID_W)
    blk_q = NA_RB * GRID_W
    return pl.pallas_call(
        _na_kernel,
        out_shape=jax.ShapeDtypeStruct((n_hp, n, LANES), BF16),
        grid=(n_hp, batch, n_rb),
        in_specs=[pl.BlockSpec((1, blk_q, LANES), lambda hp, b, r: (hp, b * n_rb + r, 0)),
                  pl.BlockSpec((1, seq, LANES), lambda hp, b, r: (hp, b, 0)),
                  pl.BlockSpec((1, seq, LANES), lambda hp, b, r: (hp, b, 0)),
                  pl.BlockSpec((1,) + bias.shape[1:], lambda hp, b, r: (hp, 0, 0, 0, 0))],
        out_specs=pl.BlockSpec((1, blk_q, LANES), lambda hp, b, r: (hp, b * n_rb + r, 0)),
        compiler_params=_cparams("parallel", "parallel", "arbitrary"),
        name="na_attn",
    )(q, k, v, bias)


def _merge_kernel(x_ref, na_ref, df_ref, ml_ref, wg_ref, bg_ref, wb_ref, wo_ref, g_ref, b_ref,
                  o_ref, *, alpha):
    x = x_ref[...]
    xb = x.astype(BF16)
    na = jnp.concatenate([na_ref[hp] for hp in range(na_ref.shape[0])], axis=1)
    branches = (na, df_ref[...], ml_ref[...])
    merged = None
    for br in range(N_BRANCH):
        z = _dot(xb, wg_ref[:, br * D_MODEL:(br + 1) * D_MODEL]) + bg_ref[:, br * D_MODEL:(br + 1) * D_MODEL]
        t = jax.nn.sigmoid(z) * _dot(branches[br], wb_ref[br])
        merged = t if merged is None else merged + t
    mix = _dot(merged.astype(BF16), wo_ref[...])
    o_ref[...] = _layer_norm(alpha * x + mix, g_ref[...], b_ref[...])


def _merge_call(x, na, df, ml, wg, bg, wb, wo, g, b, alpha, tm=256):
    n, d = x.shape
    n_hp = na.shape[0]
    const2 = lambda i: (0, 0)
    row = lambda i: (i, 0)
    return pl.pallas_call(
        functools.partial(_merge_kernel, alpha=alpha),
        out_shape=jax.ShapeDtypeStruct((n, d), F32),
        grid=(n // tm,),
        in_specs=[pl.BlockSpec((tm, d), row),
                  pl.BlockSpec((n_hp, tm, LANES), lambda i: (0, i, 0)),
                  pl.BlockSpec((tm, BRANCH_WIDTH), row),
                  pl.BlockSpec((tm, BRANCH_WIDTH), row),
                  pl.BlockSpec(wg.shape, const2),
                  pl.BlockSpec((1, N_BRANCH * d), const2),
                  pl.BlockSpec(wb.shape, lambda i: (0, 0, 0)),
                  pl.BlockSpec(wo.shape, const2),
                  pl.BlockSpec((1, d), const2),
                  pl.BlockSpec((1, d), const2)],
        out_specs=pl.BlockSpec((tm, d), row),
        compiler_params=_cparams("parallel"),
        name="merge_ln",
    )(x, na, df, ml, wg, bg.reshape(1, -1), wb, wo, g.reshape(1, d), b.reshape(1, d))


def _ffn_kernel(x_ref, w1_ref, w2_ref, g_ref, b_ref, o_ref, *, alpha, d_ff, n_chunks):
    x = x_ref[...]
    xb = x.astype(BF16)
    fc = d_ff // n_chunks
    acc = None
    for c in range(n_chunks):
        gate = _dot(xb, w1_ref[:, c * fc:(c + 1) * fc])
        up = _dot(xb, w1_ref[:, d_ff + c * fc:d_ff + (c + 1) * fc])
        hid = (gate * jax.nn.sigmoid(gate) * up).astype(BF16)
        o = _dot(hid, w2_ref[c * fc:(c + 1) * fc, :])
        acc = o if acc is None else acc + o
    o_ref[...] = _layer_norm(alpha * x + acc, g_ref[...], b_ref[...])


def _ffn_call(x, w1, w2, g, b, alpha, tm=256, n_chunks=2):
    n, d = x.shape
    d_ff = w2.shape[0]
    const2 = lambda i: (0, 0)
    row = lambda i: (i, 0)
    return pl.pallas_call(
        functools.partial(_ffn_kernel, alpha=alpha, d_ff=d_ff, n_chunks=n_chunks),
        out_shape=jax.ShapeDtypeStruct((n, d), F32),
        grid=(n // tm,),
        in_specs=[pl.BlockSpec((tm, d), row),
                  pl.BlockSpec(w1.shape, const2),
                  pl.BlockSpec(w2.shape, const2),
                  pl.BlockSpec((1, d), const2),
                  pl.BlockSpec((1, d), const2)],
        out_specs=pl.BlockSpec((tm, d), row),
        compiler_params=_cparams("parallel"),
        name="ffn_ln",
    )(x, w1, w2, g.reshape(1, d), b.reshape(1, d))


def _rope_tables(seq, rot_dim, lane_period, lane_lo):
    half = rot_dim // 2
    inv_freq = jnp.exp(-math.log(ROPE_THETA) * jnp.arange(half, dtype=F32) / half)
    ang = jnp.arange(seq, dtype=F32)[:, None] * inv_freq[None, :]
    cos, sin = jnp.cos(ang), jnp.sin(ang)
    d = (np.arange(LANES) % lane_period) - lane_lo
    is1 = (d >= 0) & (d < half)
    is2 = (d >= half) & (d < rot_dim)
    fidx = np.where(is1, d, np.where(is2, d - half, 0))
    cos_l, sin_l = cos[:, fidx], sin[:, fidx]
    c = jnp.where(is1 | is2, cos_l, 1.0)
    sa = jnp.where(is1, -sin_l, 0.0)
    sb = jnp.where(is2, sin_l, 0.0)
    return c, sa, sb


def _pad_cols(w, n_out, groups):
    out = jnp.zeros((w.shape[0], n_out), w.dtype)
    for dst, src, wd in groups:
        out = out.at[:, dst:dst + wd].set(w[:, src:src + wd])
    return out


def _layer_params(l, w_in, w_mla_qb, w_mla_kvb):
    d = w_in.shape[1]
    w = jnp.zeros((d, C_END), F32)
    w = w.at[:, :C_KR].set(w_in[l, :, :C_KR])
    w = w.at[:, C_KR + MLA_NOPE_DIM:C_KR + MLA_NOPE_DIM + MLA_ROPE_DIM].set(w_in[l, :, C_KR:N_PROJ])
    wg = w_in[l, :, N_PROJ:]
    qd = MLA_NOPE_DIM + MLA_ROPE_DIM
    n_pad = MLA_HEADS * LANES
    wq = _pad_cols(w_mla_qb[l], n_pad, [(h * LANES, h * qd, qd) for h in range(MLA_HEADS)])
    kvd = MLA_NOPE_DIM + MLA_V_DIM
    wk = _pad_cols(w_mla_kvb[l], n_pad, [(h * LANES, h * kvd, MLA_NOPE_DIM) for h in range(MLA_HEADS)])
    wv = _pad_cols(w_mla_kvb[l], n_pad,
                   [(h * LANES + (h % 2) * MLA_V_DIM, h * kvd + MLA_NOPE_DIM, MLA_V_DIM)
                    for h in range(MLA_HEADS)])
    return w.astype(BF16), wg.astype(BF16), wq.astype(BF16), wk.astype(BF16), wv.astype(BF16)


def _v_ones_row():
    lane = np.arange(MLA_HEADS * LANES)
    h, r = lane // LANES, lane % LANES
    ones = np.where(h % 2 == 0, r >= MLA_V_DIM, r < MLA_V_DIM)
    return jnp.asarray(ones.astype(np.float32)).reshape(1, -1)


def kernel(x, ln_in_g, ln_in_b, w_in, b_gate, na_rpb, diff_lambda, diff_subln_g, mla_q_norm_g,
           mla_kv_norm_g, w_mla_qb, w_mla_kvb, w_branch, w_out, ln1_g, ln1_b, w_ffn_in, w_ffn_out,
           ln2_g, ln2_b):
    batch, seq, d = x.shape
    depth = w_in.shape[0]
    assert d == D_MODEL and seq == GRID_W * GRID_W and w_in.shape[2] == N_PROJ + N_BRANCH * D_MODEL
    alpha = (2 * depth) ** 0.25

    dtabs = _rope_tables(seq, DIFF_ROT_DIM, DIFF_HEAD_DIM, 0)
    mtabs = _rope_tables(seq, MLA_ROPE_DIM, LANES, MLA_NOPE_DIM)
    vone = _v_ones_row()
    na_neg = jnp.asarray(_na_window_mask())

    h = _ln_call(x.reshape(batch * seq, d), ln_in_g, ln_in_b)
    for l in range(depth):
        lam_init = 0.8 - 0.6 * math.exp(-0.3 * l)
        w, wg, wq, wk, wv = _layer_params(l, w_in, w_mla_qb, w_mla_kvb)
        (naq, nak, nav, dfq, dfk, dfv, mlq, mlk, mlv) = _proj_call(
            h, w, wq, wk, wv, mla_q_norm_g[l].reshape(1, -1), mla_kv_norm_g[l].reshape(1, -1),
            vone, dtabs, mtabs, seq)
        na_out = _na_call(naq, nak, nav, _na_bias_tables(na_rpb[l], na_neg), batch, seq)
        df_out = _diff_attn_call(dfq, dfk, dfv, diff_lambda[l], diff_subln_g[l], lam_init, batch, seq)
        ml_out = _mla_attn_call(mlq, mlk, mlv, batch, seq)
        h = _merge_call(h, na_out, df_out, ml_out, wg, b_gate[l], w_branch[l].astype(BF16),
                        w_out[l].astype(BF16), ln1_g[l], ln1_b[l], alpha)
        h = _ffn_call(h, w_ffn_in[l].astype(BF16), w_ffn_out[l].astype(BF16), ln2_g[l], ln2_b[l], alpha)
    return h.reshape(batch, seq, d)
```

```python
import functools
import math

import numpy as np
import jax
import jax.numpy as jnp
from jax import lax
from jax.experimental import pallas as pl
from jax.experimental.pallas import tpu as pltpu

F32 = jnp.float32
BF16 = jnp.bfloat16

LANES = 128
VMEM_LIMIT_BYTES = 56 * 1024 * 1024

D_MODEL = 1024
GRID_W = 64
NA_HEADS = 8
NA_HEAD_DIM = 64
NA_WIN_ROWS = 8
NA_WIN_COLS = 16
DIFF_HEADS = 4
DIFF_HEAD_DIM = 64
DIFF_ROT_DIM = DIFF_HEAD_DIM // 4
MLA_HEADS = 8
MLA_NOPE_DIM = 64
MLA_ROPE_DIM = 32
MLA_V_DIM = 64
MLA_Q_RANK = 384
MLA_KV_RANK = 256
N_BRANCH = 3
BRANCH_WIDTH = 512
ROPE_THETA = 500000.0
LN_EPS = 1e-5
RMS_EPS = 1e-6
LOG2E = math.log2(math.e)

C_NAQ, C_NAK, C_NAV = 0, 512, 1024
C_DFQ, C_DFK, C_DFV = 1536, 2048, 2560
C_CQ = 3072
C_CKV = C_CQ + MLA_Q_RANK
C_KR = C_CKV + MLA_KV_RANK
C_END = C_KR + LANES
N_PROJ = 3 * 512 + 3 * 512 + MLA_Q_RANK + MLA_KV_RANK + MLA_ROPE_DIM

NA_RB = 8
NA_PAIR_Q = 2 * GRID_W
NA_BAND = 10 * GRID_W
NA_SLAB_ROWS = NA_RB + NA_WIN_ROWS
NA_TABLES = (((3, 0), (3, 1)),
             ((7, 0), (6, 0)),
             ((5, 0), (4, 0)),
             ((3, 2), (2, 2)),
             ((1, 2), (0, 2)))
NA_OFF_FIRST, NA_TAB_FIRST = (0, 0, 0, 2), (1, 2, 0, 0)
NA_OFF_LAST, NA_TAB_LAST = (4, 6, 6, 6), (0, 0, 3, 4)


def _cparams(*sem):
    return pltpu.CompilerParams(dimension_semantics=sem, vmem_limit_bytes=VMEM_LIMIT_BYTES)


def _dot(a, b):
    return jnp.dot(a, b, preferred_element_type=F32)


def _dot_nt(a, b):
    return lax.dot_general(a, b, (((1,), (1,)), ((), ())), preferred_element_type=F32)


def _layer_norm(x, g, b):
    mu = jnp.mean(x, axis=-1, keepdims=True)
    xc = x - mu
    var = jnp.mean(xc * xc, axis=-1, keepdims=True)
    return xc * lax.rsqrt(var + LN_EPS) * g + b


def _rms_norm(x, g):
    return x * lax.rsqrt(jnp.mean(x * x, axis=-1, keepdims=True) + RMS_EPS) * g


def _ln_kernel(x_ref, g_ref, b_ref, o_ref):
    o_ref[...] = _layer_norm(x_ref[...], g_ref[...], b_ref[...])


def _ln_call(x, g, b, tm=512):
    n, d = x.shape
    return pl.pallas_call(
        _ln_kernel,
        out_shape=jax.ShapeDtypeStruct((n, d), F32),
        grid=(n // tm,),
        in_specs=[pl.BlockSpec((tm, d), lambda i: (i, 0)),
                  pl.BlockSpec((1, d), lambda i: (0, 0)),
                  pl.BlockSpec((1, d), lambda i: (0, 0))],
        out_specs=pl.BlockSpec((tm, d), lambda i: (i, 0)),
        compiler_params=_cparams("parallel"),
        name="ln_in",
    )(x, g.reshape(1, d), b.reshape(1, d))


def _rope_lanes(x, c, sa, sb, shift):
    return x * c + pltpu.roll(x, LANES - shift, 1) * sa + pltpu.roll(x, shift, 1) * sb


def _proj_kernel(x_ref, w_ref, wq_ref, wk_ref, wv_ref, qg_ref, kvg_ref, vone_ref,
                 dc_ref, dsa_ref, dsb_ref, mc_ref, msa_ref, msb_ref,
                 naq_ref, nak_ref, nav_ref, dfq_ref, dfk_ref, dfv_ref,
                 mlq_ref, mlk_ref, mlv_ref, *, na_qscale, df_qscale, ml_qscale):
    xb = x_ref[...].astype(BF16)

    def mm(lo, hi):
        return _dot(xb, w_ref[:, lo:hi])

    n_hp = naq_ref.shape[0]
    y = mm(C_NAQ, C_NAQ + 512) * na_qscale
    for hp in range(n_hp):
        naq_ref[hp] = y[:, hp * LANES:(hp + 1) * LANES].astype(BF16)
    y = mm(C_NAK, C_NAK + 512)
    for hp in range(n_hp):
        nak_ref[hp] = y[:, hp * LANES:(hp + 1) * LANES].astype(BF16)
    y = mm(C_NAV, C_NAV + 512)
    for hp in range(n_hp):
        nav_ref[hp] = y[:, hp * LANES:(hp + 1) * LANES].astype(BF16)

    dc, dsa, dsb = dc_ref[...], dsa_ref[...], dsb_ref[...]
    half = DIFF_ROT_DIM // 2
    y = mm(C_DFQ, C_DFQ + 512)
    for h in range(DIFF_HEADS):
        blk = _rope_lanes(y[:, h * LANES:(h + 1) * LANES], dc, dsa, dsb, half)
        dfq_ref[:, h * LANES:(h + 1) * LANES] = (blk * df_qscale).astype(BF16)
    y = mm(C_DFK, C_DFK + 512)
    for h in range(DIFF_HEADS):
        blk = _rope_lanes(y[:, h * LANES:(h + 1) * LANES], dc, dsa, dsb, half)
        dfk_ref[:, h * LANES:(h + 1) * LANES] = blk.astype(BF16)
    dfv_ref[...] = mm(C_DFV, C_DFV + 512).astype(BF16)

    mc, msa, msb = mc_ref[...], msa_ref[...], msb_ref[...]
    mhalf = MLA_ROPE_DIM // 2
    cq = _rms_norm(mm(C_CQ, C_CKV), qg_ref[...]).astype(BF16)
    yq = _dot(cq, wq_ref[...])
    for h in range(MLA_HEADS):
        blk = _rope_lanes(yq[:, h * LANES:(h + 1) * LANES], mc, msa, msb, mhalf)
        mlq_ref[:, h * LANES:(h + 1) * LANES] = (blk * ml_qscale).astype(BF16)
    ckv = _rms_norm(mm(C_CKV, C_KR), kvg_ref[...]).astype(BF16)
    kr = mm(C_KR, C_END)
    yk = _dot(ckv, wk_ref[...])
    for h in range(MLA_HEADS):
        blk = _rope_lanes(yk[:, h * LANES:(h + 1) * LANES] + kr, mc, msa, msb, mhalf)
        mlk_ref[:, h * LANES:(h + 1) * LANES] = blk.astype(BF16)
    mlv_ref[...] = (_dot(ckv, wv_ref[...]) + vone_ref[...]).astype(BF16)


def _proj_call(x, w, wq, wk, wv, qg, kvg, vone, dtabs, mtabs, seq, tm=256):
    n, d = x.shape
    n_pos_blocks = seq // tm
    const = lambda i: (0, 0)
    row = lambda i: (i, 0)
    pos = lambda i: (i % n_pos_blocks, 0)
    hp3 = lambda i: (0, i, 0)
    n_hp = NA_HEADS * NA_HEAD_DIM // LANES
    kern = functools.partial(
        _proj_kernel,
        na_qscale=NA_HEAD_DIM ** -0.5 * LOG2E,
        df_qscale=DIFF_HEAD_DIM ** -0.5 * LOG2E,
        ml_qscale=(MLA_NOPE_DIM + MLA_ROPE_DIM) ** -0.5 * LOG2E)
    tab_spec = pl.BlockSpec((tm, LANES), pos)
    na_shape = jax.ShapeDtypeStruct((n_hp, n, LANES), BF16)
    return pl.pallas_call(
        kern,
        out_shape=(na_shape, na_shape, na_shape,
                   jax.ShapeDtypeStruct((n, 512), BF16),
                   jax.ShapeDtypeStruct((n, 512), BF16),
                   jax.ShapeDtypeStruct((n, 512), BF16),
                   jax.ShapeDtypeStruct((n, MLA_HEADS * LANES), BF16),
                   jax.ShapeDtypeStruct((n, MLA_HEADS * LANES), BF16),
                   jax.ShapeDtypeStruct((n, MLA_HEADS * LANES), BF16)),
        grid=(n // tm,),
        in_specs=[pl.BlockSpec((tm, d), row),
                  pl.BlockSpec(w.shape, const),
                  pl.BlockSpec(wq.shape, const),
                  pl.BlockSpec(wk.shape, const),
                  pl.BlockSpec(wv.shape, const),
                  pl.BlockSpec(qg.shape, const),
                  pl.BlockSpec(kvg.shape, const),
                  pl.BlockSpec(vone.shape, const),
                  tab_spec, tab_spec, tab_spec, tab_spec, tab_spec, tab_spec],
        out_specs=(pl.BlockSpec((n_hp, tm, LANES), hp3),
                   pl.BlockSpec((n_hp, tm, LANES), hp3),
                   pl.BlockSpec((n_hp, tm, LANES), hp3),
                   pl.BlockSpec((tm, 512), row),
                   pl.BlockSpec((tm, 512), row),
                   pl.BlockSpec((tm, 512), row),
                   pl.BlockSpec((tm, MLA_HEADS * LANES), row),
                   pl.BlockSpec((tm, MLA_HEADS * LANES), row),
                   pl.BlockSpec((tm, MLA_HEADS * LANES), row)),
        compiler_params=_cparams("parallel"),
        name="proj",
    )(x, w, wq, wk, wv, qg, kvg, vone, *dtabs, *mtabs)


def _attend_online(streams, tk):
    n_str = len(streams)
    tq = streams[0][0].shape[0]
    n_chunks = streams[0][1].shape[0] // tk
    n_lt = tk // LANES

    def scores(i, c):
        q, k_ref, k_lo = streams[i][:3]
        return _dot_nt(q, k_ref[c * tk:(c + 1) * tk, k_lo:k_lo + LANES])

    nxt = [scores(i, 0) for i in range(n_str)]
    m = [None] * n_str
    acc = [None] * n_str
    for c in range(n_chunks):
        for i in range(n_str):
            _, _, _, v_ref, v_lo, v_w = streams[i]
            s = nxt[i]
            if c + 1 < n_chunks:
                nxt[i] = scores(i, c + 1)
            mw = s[:, :LANES]
            for j in range(1, n_lt):
                mw = jnp.maximum(mw, s[:, j * LANES:(j + 1) * LANES])
            mc = jnp.broadcast_to(jnp.max(mw, axis=-1, keepdims=True), (tq, LANES))
            m_new = mc if m[i] is None else jnp.maximum(m[i], mc)
            p = jnp.concatenate(
                [jnp.exp2(s[:, j * LANES:(j + 1) * LANES] - m_new) for j in range(n_lt)],
                axis=1).astype(BF16)
            o = _dot(p, v_ref[c * tk:(c + 1) * tk, v_lo:v_lo + v_w])
            if acc[i] is None:
                acc[i] = o
            else:
                alpha = jnp.exp2(m[i] - m_new)
                acc[i] = jnp.concatenate(
                    [acc[i][:, j * LANES:(j + 1) * LANES] * alpha for j in range(v_w // LANES)],
                    axis=1) + o
            m[i] = m_new
    return acc


def _diff_attn_kernel(q_ref, k_ref, v_ref, lam_ref, g_ref, o_ref, vx_ref, *, lam_init, tk):
    @pl.when(pl.program_id(2) == 0)
    def _():
        vx_ref[:, :LANES] = v_ref[...]
        vx_ref[:, LANES:] = jnp.ones((v_ref.shape[0], LANES), BF16)

    q = q_ref[...]
    lane = lax.broadcasted_iota(jnp.int32, q.shape, 1)
    zero = jnp.zeros_like(q)
    q1 = jnp.where(lane < DIFF_HEAD_DIM, q, zero)
    q2 = jnp.where(lane >= DIFF_HEAD_DIM, q, zero)
    acc1, acc2 = _attend_online(
        [(q1, k_ref, 0, vx_ref, 0, 2 * LANES), (q2, k_ref, 0, vx_ref, 0, 2 * LANES)], tk)
    a1 = acc1[:, :LANES] / acc1[:, LANES:]
    a2 = acc2[:, :LANES] / acc2[:, LANES:]

    lv = lam_ref[...]
    d1 = jnp.sum(lv[0:1] * lv[1:2], axis=-1, keepdims=True)
    d2 = jnp.sum(lv[2:3] * lv[3:4], axis=-1, keepdims=True)
    lam = jnp.exp(d1) - jnp.exp(d2) + lam_init
    o = a1 - lam * a2
    o_ref[...] = (_rms_norm(o, g_ref[...]) * (1.0 - lam_init)).astype(o_ref.dtype)


def _diff_attn_call(q, k, v, lam_vecs, subln_g, lam_init, batch, seq, tq=512, tk=512):
    n = q.shape[0]
    nq = seq // tq
    kern = functools.partial(_diff_attn_kernel, lam_init=lam_init, tk=tk)
    return pl.pallas_call(
        kern,
        out_shape=jax.ShapeDtypeStruct((n, DIFF_HEADS * LANES), BF16),
        grid=(batch, DIFF_HEADS, nq),
        in_specs=[pl.BlockSpec((tq, LANES), lambda b, h, i: (b * nq + i, h)),
                  pl.BlockSpec((seq, LANES), lambda b, h, i: (b, h)),
                  pl.BlockSpec((seq, LANES), lambda b, h, i: (b, h)),
                  pl.BlockSpec(lam_vecs.shape, lambda b, h, i: (0, 0)),
                  pl.BlockSpec((1, LANES), lambda b, h, i: (0, 0))],
        out_specs=pl.BlockSpec((tq, LANES), lambda b, h, i: (b * nq + i, h)),
        scratch_shapes=[pltpu.VMEM((seq, 2 * LANES), BF16)],
        compiler_params=_cparams("parallel", "parallel", "arbitrary"),
        name="diff_attn",
    )(q, k, v, lam_vecs, subln_g.reshape(1, LANES))


def _mla_attn_kernel(q_ref, k_ref, v_ref, o_ref, *, tk):
    acc_e, acc_o = _attend_online(
        [(q_ref[:, :LANES], k_ref, 0, v_ref, 0, LANES),
         (q_ref[:, LANES:], k_ref, LANES, v_ref, LANES, LANES)], tk)
    lane = lax.broadcasted_iota(jnp.int32, acc_e.shape, 1)
    lo = lane < MLA_V_DIM
    num = jnp.where(lo, acc_e, acc_o)
    den = jnp.where(lo, pltpu.roll(acc_e, MLA_V_DIM, 1), pltpu.roll(acc_o, MLA_V_DIM, 1))
    o_ref[...] = (num / den).astype(o_ref.dtype)


def _mla_attn_call(q, k, v, batch, seq, tq=512, tk=512):
    n = q.shape[0]
    nq = seq // tq
    n_pairs = MLA_HEADS // 2
    kern = functools.partial(_mla_attn_kernel, tk=tk)
    return pl.pallas_call(
        kern,
        out_shape=jax.ShapeDtypeStruct((n, n_pairs * LANES), BF16),
        grid=(batch, n_pairs, nq),
        in_specs=[pl.BlockSpec((tq, 2 * LANES), lambda b, h, i: (b * nq + i, h)),
                  pl.BlockSpec((seq, 2 * LANES), lambda b, h, i: (b, h)),
                  pl.BlockSpec((seq, 2 * LANES), lambda b, h, i: (b, h))],
        out_specs=pl.BlockSpec((tq, LANES), lambda b, h, i: (b * nq + i, h)),
        compiler_params=_cparams("parallel", "parallel", "arbitrary"),
        name="mla_attn",
    )(q, k, v)


def _na_window_mask():
    neg = np.full((len(NA_TABLES), NA_PAIR_Q, NA_BAND), -np.inf, np.float32)
    c = np.arange(GRID_W)[:, None]
    cp = np.arange(GRID_W)[None, :]
    wstart = np.clip(c - NA_WIN_COLS // 2, 0, GRID_W - NA_WIN_COLS)
    col_ok = (cp >= wstart) & (cp < wstart + NA_WIN_COLS)
    for t, rows in enumerate(NA_TABLES):
        for r, (_, pos) in enumerate(rows):
            for jb in range(NA_BAND // GRID_W):
                if 0 <= jb - pos < NA_WIN_ROWS:
                    neg[t, r * GRID_W:(r + 1) * GRID_W, jb * GRID_W:(jb + 1) * GRID_W] = np.where(col_ok, 0.0, -np.inf)
    return neg


def _na_bias_kernel(p_ref, neg_ref, o_ref):
    for t, rows in enumerate(NA_TABLES):
        for r, (dr0, pos) in enumerate(rows):
            for u in range(NA_BAND // LANES):
                dr = max(dr0 - pos + 2 * u, 0)
                src = p_ref[0, dr % 2, dr // 2:dr // 2 + 1, :]
                tile = pltpu.roll(jnp.broadcast_to(src, (GRID_W, LANES)),
                                  LANES - (NA_WIN_COLS - 1), 1, stride=1, stride_axis=0)
                qs = slice(r * GRID_W, (r + 1) * GRID_W)
                ks = slice(u * LANES, (u + 1) * LANES)
                o_ref[0, t, qs, ks] = tile + neg_ref[t, qs, ks]


def _na_bias_tables(rpb_l, neg):
    h, n_dr, n_dc = rpb_l.shape
    n_rows = 2 * (n_dr // 2 + 2)
    t_pad = jnp.zeros((h, n_rows, GRID_W), F32).at[:, :n_dr, :n_dc].set(rpb_l * LOG2E)
    p_even = t_pad.reshape(h, n_rows // 2, LANES)
    p_odd = jnp.concatenate([t_pad[:, 1:], jnp.zeros((h, 1, GRID_W), F32)], axis=1).reshape(h, n_rows // 2, LANES)
    pairs = jnp.stack([p_even, p_odd], axis=1)
    out = pl.pallas_call(
        _na_bias_kernel,
        out_shape=jax.ShapeDtypeStruct((h,) + neg.shape, F32),
        grid=(h,),
        in_specs=[pl.BlockSpec((1,) + pairs.shape[1:], lambda i: (i, 0, 0, 0)),
                  pl.BlockSpec(neg.shape, lambda i: (0, 0, 0))],
        out_specs=pl.BlockSpec((1,) + neg.shape, lambda i: (i, 0, 0, 0)),
        compiler_params=_cparams("parallel"),
        name="na_bias",
    )(pairs, neg)
    return out.reshape((h // 2, 2) + neg.shape)


def _na_kernel(q_ref, k_ref, v_ref, bias_ref, o_ref):
    rb = pl.program_id(2)
    n_rb = pl.num_programs(2)
    start = jnp.clip(NA_RB * rb - NA_WIN_ROWS // 2, 0, GRID_W - NA_SLAB_ROWS)
    first = rb == 0
    last = rb == n_rb - 1
    lane = lax.broadcasted_iota(jnp.int32, (NA_PAIR_Q, LANES), 1)
    lo = lane < NA_HEAD_DIM
    for p in range(NA_RB // 2):
        off = jnp.where(first, NA_OFF_FIRST[p], jnp.where(last, NA_OFF_LAST[p], 2 * p))
        tab = jnp.where(first, NA_TAB_FIRST[p], jnp.where(last, NA_TAB_LAST[p], 0))
        tok0 = pl.multiple_of((start + off) * GRID_W, GRID_W)
        kb = k_ref[0, pl.ds(tok0, NA_BAND), :]
        vb = v_ref[0, pl.ds(tok0, NA_BAND), :]
        qp = q_ref[0, p * NA_PAIR_Q:(p + 1) * NA_PAIR_Q, :]
        zero = jnp.zeros_like(qp)
        outs = []
        for hh in range(2):
            qm = jnp.where(lo if hh == 0 else jnp.logical_not(lo), qp, zero)
            s = _dot_nt(qm, kb) + bias_ref[0, hh, tab]
            m = jnp.max(s, axis=-1, keepdims=True)
            e = jnp.exp2(s - m)
            l = jnp.sum(e, axis=-1, keepdims=True)
            outs.append(_dot(e.astype(BF16), vb) / l)
        o_ref[0, p * NA_PAIR_Q:(p + 1) * NA_PAIR_Q, :] = jnp.where(lo, outs[0], outs[1]).astype(o_ref.dtype)


def _na_call(q, k, v, bias, batch, seq):
    n_hp, n, _ = q.shape
    n_rb = seq // (NA_RB * GRID_W)
    blk_q = NA_RB * GRID_W
    return pl.pallas_call(
        _na_kernel,
        out_shape=jax.ShapeDtypeStruct((n_hp, n, LANES), BF16),
        grid=(n_hp, batch, n_rb),
        in_specs=[pl.BlockSpec((1, blk_q, LANES), lambda hp, b, r: (hp, b * n_rb + r, 0)),
                  pl.BlockSpec((1, seq, LANES), lambda hp, b, r: (hp, b, 0)),
                  pl.BlockSpec((1, seq, LANES), lambda hp, b, r: (hp, b, 0)),
                  pl.BlockSpec((1,) + bias.shape[1:], lambda hp, b, r: (hp, 0, 0, 0, 0))],
        out_specs=pl.BlockSpec((1, blk_q, LANES), lambda hp, b, r: (hp, b * n_rb + r, 0)),
        compiler_params=_cparams("parallel", "parallel", "arbitrary"),
        name="na_attn",
    )(q, k, v, bias)


def _merge_kernel(x_ref, na_ref, df_ref, ml_ref, wg_ref, bg_ref, wb_ref, wo_ref, g_ref, b_ref,
                  o_ref, *, alpha):
    x = x_ref[...]
    xb = x.astype(BF16)
    na = jnp.concatenate([na_ref[hp] for hp in range(na_ref.shape[0])], axis=1)
    branches = (na, df_ref[...], ml_ref[...])
    merged = None
    for br in range(N_BRANCH):
        z = _dot(xb, wg_ref[:, br * D_MODEL:(br + 1) * D_MODEL]) + bg_ref[:, br * D_MODEL:(br + 1) * D_MODEL]
        t = jax.nn.sigmoid(z) * _dot(branches[br], wb_ref[br])
        merged = t if merged is None else merged + t
    mix = _dot(merged.astype(BF16), wo_ref[...])
    o_ref[...] = _layer_norm(alpha * x + mix, g_ref[...], b_ref[...])


def _merge_call(x, na, df, ml, wg, bg, wb, wo, g, b, alpha, tm=256):
    n, d = x.shape
    n_hp = na.shape[0]
    const2 = lambda i: (0, 0)
    row = lambda i: (i, 0)
    return pl.pallas_call(
        functools.partial(_merge_kernel, alpha=alpha),
        out_shape=jax.ShapeDtypeStruct((n, d), F32),
        grid=(n // tm,),
        in_specs=[pl.BlockSpec((tm, d), row),
                  pl.BlockSpec((n_hp, tm, LANES), lambda i: (0, i, 0)),
                  pl.BlockSpec((tm, BRANCH_WIDTH), row),
                  pl.BlockSpec((tm, BRANCH_WIDTH), row),
                  pl.BlockSpec(wg.shape, const2),
                  pl.BlockSpec((1, N_BRANCH * d), const2),
                  pl.BlockSpec(wb.shape, lambda i: (0, 0, 0)),
                  pl.BlockSpec(wo.shape, const2),
                  pl.BlockSpec((1, d), const2),
                  pl.BlockSpec((1, d), const2)],
        out_specs=pl.BlockSpec((tm, d), row),
        compiler_params=_cparams("parallel"),
        name="merge_ln",
    )(x, na, df, ml, wg, bg.reshape(1, -1), wb, wo, g.reshape(1, d), b.reshape(1, d))


def _ffn_kernel(x_ref, w1_ref, w2_ref, g_ref, b_ref, o_ref, *, alpha, d_ff, n_chunks):
    x = x_ref[...]
    xb = x.astype(BF16)
    fc = d_ff // n_chunks
    acc = None
    for c in range(n_chunks):
        gate = _dot(xb, w1_ref[:, c * fc:(c + 1) * fc])
        up = _dot(xb, w1_ref[:, d_ff + c * fc:d_ff + (c + 1) * fc])
        hid = (gate * jax.nn.sigmoid(gate) * up).astype(BF16)
        o = _dot(hid, w2_ref[c * fc:(c + 1) * fc, :])
        acc = o if acc is None else acc + o
    o_ref[...] = _layer_norm(alpha * x + acc, g_ref[...], b_ref[...])


def _ffn_call(x, w1, w2, g, b, alpha, tm=256, n_chunks=2):
    n, d = x.shape
    d_ff = w2.shape[0]
    const2 = lambda i: (0, 0)
    row = lambda i: (i, 0)
    return pl.pallas_call(
        functools.partial(_ffn_kernel, alpha=alpha, d_ff=d_ff, n_chunks=n_chunks),
        out_shape=jax.ShapeDtypeStruct((n, d), F32),
        grid=(n // tm,),
        in_specs=[pl.BlockSpec((tm, d), row),
                  pl.BlockSpec(w1.shape, const2),
                  pl.BlockSpec(w2.shape, const2),
                  pl.BlockSpec((1, d), const2),
                  pl.BlockSpec((1, d), const2)],
        out_specs=pl.BlockSpec((tm, d), row),
        compiler_params=_cparams("parallel"),
        name="ffn_ln",
    )(x, w1, w2, g.reshape(1, d), b.reshape(1, d))


def _rope_tables(seq, rot_dim, lane_period, lane_lo):
    half = rot_dim // 2
    inv_freq = jnp.exp(-math.log(ROPE_THETA) * jnp.arange(half, dtype=F32) / half)
    ang = jnp.arange(seq, dtype=F32)[:, None] * inv_freq[None, :]
    cos, sin = jnp.cos(ang), jnp.sin(ang)
    d = (np.arange(LANES) % lane_period) - lane_lo
    is1 = (d >= 0) & (d < half)
    is2 = (d >= half) & (d < rot_dim)
    fidx = np.where(is1, d, np.where(is2, d - half, 0))
    cos_l, sin_l = cos[:, fidx], sin[:, fidx]
    c = jnp.where(is1 | is2, cos_l, 1.0)
    sa = jnp.where(is1, -sin_l, 0.0)
    sb = jnp.where(is2, sin_l, 0.0)
    return c, sa, sb


def _pad_cols(w, n_out, groups):
    out = jnp.zeros((w.shape[0], n_out), w.dtype)
    for dst, src, wd in groups:
        out = out.at[:, dst:dst + wd].set(w[:, src:src + wd])
    return out


def _layer_params(l, w_in, w_mla_qb, w_mla_kvb):
    d = w_in.shape[1]
    w = jnp.zeros((d, C_END), F32)
    w = w.at[:, :C_KR].set(w_in[l, :, :C_KR])
    w = w.at[:, C_KR + MLA_NOPE_DIM:C_KR + MLA_NOPE_DIM + MLA_ROPE_DIM].set(w_in[l, :, C_KR:N_PROJ])
    wg = w_in[l, :, N_PROJ:]
    qd = MLA_NOPE_DIM + MLA_ROPE_DIM
    n_pad = MLA_HEADS * LANES
    wq = _pad_cols(w_mla_qb[l], n_pad, [(h * LANES, h * qd, qd) for h in range(MLA_HEADS)])
    kvd = MLA_NOPE_DIM + MLA_V_DIM
    wk = _pad_cols(w_mla_kvb[l], n_pad, [(h * LANES, h * kvd, MLA_NOPE_DIM) for h in range(MLA_HEADS)])
    wv = _pad_cols(w_mla_kvb[l], n_pad,
                   [(h * LANES + (h % 2) * MLA_V_DIM, h * kvd + MLA_NOPE_DIM, MLA_V_DIM)
                    for h in range(MLA_HEADS)])
    return w.astype(BF16), wg.astype(BF16), wq.astype(BF16), wk.astype(BF16), wv.astype(BF16)


def _v_ones_row():
    lane = np.arange(MLA_HEADS * LANES)
    h, r = lane // LANES, lane % LANES
    ones = np.where(h % 2 == 0, r >= MLA_V_DIM, r < MLA_V_DIM)
    return jnp.asarray(ones.astype(np.float32)).reshape(1, -1)


def kernel(x, ln_in_g, ln_in_b, w_in, b_gate, na_rpb, diff_lambda, diff_subln_g, mla_q_norm_g,
           mla_kv_norm_g, w_mla_qb, w_mla_kvb, w_branch, w_out, ln1_g, ln1_b, w_ffn_in, w_ffn_out,
           ln2_g, ln2_b):
    batch, seq, d = x.shape
    depth = w_in.shape[0]
    assert d == D_MODEL and seq == GRID_W * GRID_W and w_in.shape[2] == N_PROJ + N_BRANCH * D_MODEL
    alpha = (2 * depth) ** 0.25

    dtabs = _rope_tables(seq, DIFF_ROT_DIM, DIFF_HEAD_DIM, 0)
    mtabs = _rope_tables(seq, MLA_ROPE_DIM, LANES, MLA_NOPE_DIM)
    vone = _v_ones_row()
    na_neg = jnp.asarray(_na_window_mask())

    h = _ln_call(x.reshape(batch * seq, d), ln_in_g, ln_in_b)
    for l in range(depth):
        lam_init = 0.8 - 0.6 * math.exp(-0.3 * l)
        w, wg, wq, wk, wv = _layer_params(l, w_in, w_mla_qb, w_mla_kvb)
        (naq, nak, nav, dfq, dfk, dfv, mlq, mlk, mlv) = _proj_call(
            h, w, wq, wk, wv, mla_q_norm_g[l].reshape(1, -1), mla_kv_norm_g[l].reshape(1, -1),
            vone, dtabs, mtabs, seq)
        na_out = _na_call(naq, nak, nav, _na_bias_tables(na_rpb[l], na_neg), batch, seq)
        df_out = _diff_attn_call(dfq, dfk, dfv, diff_lambda[l], diff_subln_g[l], lam_init, batch, seq)
        ml_out = _mla_attn_call(mlq, mlk, mlv, batch, seq)
        h = _merge_call(h, na_out, df_out, ml_out, wg, b_gate[l], w_branch[l].astype(BF16),
                        w_out[l].astype(BF16), ln1_g[l], ln1_b[l], alpha)
        h = _ffn_call(h, w_ffn_in[l].astype(BF16), w_ffn_out[l].astype(BF16), ln2_g[l], ln2_b[l], alpha)
    return h.reshape(batch, seq, d)
```

```python
import functools
import math

import numpy as np
import jax
import jax.numpy as jnp
from jax import lax
from jax.experimental import pallas as pl
from jax.experimental.pallas import tpu as pltpu

F32 = jnp.float32
BF16 = jnp.bfloat16

LANES = 128
VMEM_LIMIT_BYTES = 56 * 1024 * 1024

D_MODEL = 1024
GRID_W = 64
NA_HEADS = 8
NA_HEAD_DIM = 64
NA_WIN_ROWS = 8
NA_WIN_COLS = 16
DIFF_HEADS = 4
DIFF_HEAD_DIM = 64
DIFF_ROT_DIM = DIFF_HEAD_DIM // 4
MLA_HEADS = 8
MLA_NOPE_DIM = 64
MLA_ROPE_DIM = 32
MLA_V_DIM = 64
MLA_Q_RANK = 384
MLA_KV_RANK = 256
N_BRANCH = 3
BRANCH_WIDTH = 512
ROPE_THETA = 500000.0
LN_EPS = 1e-5
RMS_EPS = 1e-6
LOG2E = math.log2(math.e)

C_NAQ, C_NAK, C_NAV = 0, 512, 1024
C_DFQ, C_DFK, C_DFV = 1536, 2048, 2560
C_CQ = 3072
C_CKV = C_CQ + MLA_Q_RANK
C_KR = C_CKV + MLA_KV_RANK
C_END = C_KR + LANES
N_PROJ = 3 * 512 + 3 * 512 + MLA_Q_RANK + MLA_KV_RANK + MLA_ROPE_DIM

NA_RB = 8
NA_PAIR_Q = 2 * GRID_W
NA_BAND = 10 * GRID_W
NA_SLAB_ROWS = NA_RB + NA_WIN_ROWS
NA_TABLES = (((3, 0), (3, 1)),
             ((7, 0), (6, 0)),
             ((5, 0), (4, 0)),
             ((3, 2), (2, 2)),
             ((1, 2), (0, 2)))
NA_OFF_FIRST, NA_TAB_FIRST = (0, 0, 0, 2), (1, 2, 0, 0)
NA_OFF_LAST, NA_TAB_LAST = (4, 6, 6, 6), (0, 0, 3, 4)


def _cparams(*sem):
    return pltpu.CompilerParams(dimension_semantics=sem, vmem_limit_bytes=VMEM_LIMIT_BYTES)


def _resident(shape):
    nd = len(shape)
    return pl.BlockSpec(shape, lambda *_: (0,) * nd, pipeline_mode=pl.Buffered(1))


def _dot(a, b):
    return jnp.dot(a, b, preferred_element_type=F32)


def _dot_nt(a, b):
    return lax.dot_general(a, b, (((1,), (1,)), ((), ())), preferred_element_type=F32)


def _layer_norm(x, g, b):
    mu = jnp.mean(x, axis=-1, keepdims=True)
    xc = x - mu
    var = jnp.mean(xc * xc, axis=-1, keepdims=True)
    return xc * lax.rsqrt(var + LN_EPS) * g + b


def _rms_norm(x, g):
    return x * lax.rsqrt(jnp.mean(x * x, axis=-1, keepdims=True) + RMS_EPS) * g


def _ln_kernel(x_ref, g_ref, b_ref, o_ref):
    o_ref[...] = _layer_norm(x_ref[...], g_ref[...], b_ref[...])


def _ln_call(x, g, b, tm=512):
    n, d = x.shape
    return pl.pallas_call(
        _ln_kernel,
        out_shape=jax.ShapeDtypeStruct((n, d), F32),
        grid=(n // tm,),
        in_specs=[pl.BlockSpec((tm, d), lambda i: (i, 0)),
                  pl.BlockSpec((1, d), lambda i: (0, 0)),
                  pl.BlockSpec((1, d), lambda i: (0, 0))],
        out_specs=pl.BlockSpec((tm, d), lambda i: (i, 0)),
        compiler_params=_cparams("parallel"),
        name="ln_in",
    )(x, g.reshape(1, d), b.reshape(1, d))


def _rope_lanes(x, c, sa, sb, shift):
    return x * c + pltpu.roll(x, LANES - shift, 1) * sa + pltpu.roll(x, shift, 1) * sb


def _proj_kernel(x_ref, w_ref, wq_ref, wk_ref, wv_ref, qg_ref, kvg_ref, vone_ref,
                 dc_ref, dsa_ref, dsb_ref, mc_ref, msa_ref, msb_ref,
                 naq_ref, nak_ref, nav_ref, dfq_ref, dfk_ref, dfv_ref,
                 mlq_ref, mlk_ref, mlv_ref, *, na_qscale, df_qscale, ml_qscale):
    xb = x_ref[...].astype(BF16)

    def mm(lo, hi):
        return _dot(xb, w_ref[:, lo:hi])

    n_hp = naq_ref.shape[0]
    y = mm(C_NAQ, C_NAQ + 512) * na_qscale
    for hp in range(n_hp):
        naq_ref[hp] = y[:, hp * LANES:(hp + 1) * LANES].astype(BF16)
    y = mm(C_NAK, C_NAK + 512)
    for hp in range(n_hp):
        nak_ref[hp] = y[:, hp * LANES:(hp + 1) * LANES].astype(BF16)
    y = mm(C_NAV, C_NAV + 512)
    for hp in range(n_hp):
        nav_ref[hp] = y[:, hp * LANES:(hp + 1) * LANES].astype(BF16)

    dc, dsa, dsb = dc_ref[...], dsa_ref[...], dsb_ref[...]
    half = DIFF_ROT_DIM // 2
    y = mm(C_DFQ, C_DFQ + 512)
    for h in range(DIFF_HEADS):
        blk = _rope_lanes(y[:, h * LANES:(h + 1) * LANES], dc, dsa, dsb, half)
        dfq_ref[:, h * LANES:(h + 1) * LANES] = (blk * df_qscale).astype(BF16)
    y = mm(C_DFK, C_DFK + 512)
    for h in range(DIFF_HEADS):
        blk = _rope_lanes(y[:, h * LANES:(h + 1) * LANES], dc, dsa, dsb, half)
        dfk_ref[:, h * LANES:(h + 1) * LANES] = blk.astype(BF16)
    dfv_ref[...] = mm(C_DFV, C_DFV + 512).astype(BF16)

    mc, msa, msb = mc_ref[...], msa_ref[...], msb_ref[...]
    mhalf = MLA_ROPE_DIM // 2
    cq = _rms_norm(mm(C_CQ, C_CKV), qg_ref[...]).astype(BF16)
    yq = _dot(cq, wq_ref[...])
    for h in range(MLA_HEADS):
        blk = _rope_lanes(yq[:, h * LANES:(h + 1) * LANES], mc, msa, msb, mhalf)
        mlq_ref[:, h * LANES:(h + 1) * LANES] = (blk * ml_qscale).astype(BF16)
    ckv = _rms_norm(mm(C_CKV, C_KR), kvg_ref[...]).astype(BF16)
    kr = mm(C_KR, C_END)
    yk = _dot(ckv, wk_ref[...])
    for h in range(MLA_HEADS):
        blk = _rope_lanes(yk[:, h * LANES:(h + 1) * LANES] + kr, mc, msa, msb, mhalf)
        mlk_ref[:, h * LANES:(h + 1) * LANES] = blk.astype(BF16)
    mlv_ref[...] = (_dot(ckv, wv_ref[...]) + vone_ref[...]).astype(BF16)


def _proj_call(x, w, wq, wk, wv, qg, kvg, vone, dtabs, mtabs, seq, tm=512):
    n, d = x.shape
    n_pos_blocks = seq // tm
    row = lambda i: (i, 0)
    pos = lambda i: (i % n_pos_blocks, 0)
    hp3 = lambda i: (0, i, 0)
    n_hp = NA_HEADS * NA_HEAD_DIM // LANES
    kern = functools.partial(
        _proj_kernel,
        na_qscale=NA_HEAD_DIM ** -0.5 * LOG2E,
        df_qscale=DIFF_HEAD_DIM ** -0.5 * LOG2E,
        ml_qscale=(MLA_NOPE_DIM + MLA_ROPE_DIM) ** -0.5 * LOG2E)
    tab_spec = pl.BlockSpec((tm, LANES), pos)
    na_shape = jax.ShapeDtypeStruct((n_hp, n, LANES), BF16)
    return pl.pallas_call(
        kern,
        out_shape=(na_shape, na_shape, na_shape,
                   jax.ShapeDtypeStruct((n, 512), BF16),
                   jax.ShapeDtypeStruct((n, 512), BF16),
                   jax.ShapeDtypeStruct((n, 512), BF16),
                   jax.ShapeDtypeStruct((n, MLA_HEADS * LANES), BF16),
                   jax.ShapeDtypeStruct((n, MLA_HEADS * LANES), BF16),
                   jax.ShapeDtypeStruct((n, MLA_HEADS * LANES), BF16)),
        grid=(n // tm,),
        in_specs=[pl.BlockSpec((tm, d), row),
                  _resident(w.shape), _resident(wq.shape), _resident(wk.shape), _resident(wv.shape),
                  _resident(qg.shape), _resident(kvg.shape), _resident(vone.shape),
                  tab_spec, tab_spec, tab_spec, tab_spec, tab_spec, tab_spec],
        out_specs=(pl.BlockSpec((n_hp, tm, LANES), hp3),
                   pl.BlockSpec((n_hp, tm, LANES), hp3),
                   pl.BlockSpec((n_hp, tm, LANES), hp3),
                   pl.BlockSpec((tm, 512), row),
                   pl.BlockSpec((tm, 512), row),
                   pl.BlockSpec((tm, 512), row),
                   pl.BlockSpec((tm, MLA_HEADS * LANES), row),
                   pl.BlockSpec((tm, MLA_HEADS * LANES), row),
                   pl.BlockSpec((tm, MLA_HEADS * LANES), row)),
        compiler_params=_cparams("parallel"),
        name="proj",
    )(x, w, wq, wk, wv, qg, kvg, vone, *dtabs, *mtabs)


def _attend_online(streams, tk):
    n_str = len(streams)
    tq = streams[0][0].shape[0]
    n_chunks = streams[0][1].shape[0] // tk
    n_lt = tk // LANES

    def scores(i, c):
        q, k_ref, k_lo = streams[i][:3]
        return _dot_nt(q, k_ref[c * tk:(c + 1) * tk, k_lo:k_lo + LANES])

    nxt = [scores(i, 0) for i in range(n_str)]
    m = [None] * n_str
    acc = [None] * n_str
    for c in range(n_chunks):
        for i in range(n_str):
            _, _, _, v_ref, v_lo, v_w = streams[i]
            s = nxt[i]
            if c + 1 < n_chunks:
                nxt[i] = scores(i, c + 1)
            mw = s[:, :LANES]
            for j in range(1, n_lt):
                mw = jnp.maximum(mw, s[:, j * LANES:(j + 1) * LANES])
            mc = jnp.broadcast_to(jnp.max(mw, axis=-1, keepdims=True), (tq, LANES))
            m_new = mc if m[i] is None else jnp.maximum(m[i], mc)
            p = jnp.concatenate(
                [jnp.exp2(s[:, j * LANES:(j + 1) * LANES] - m_new) for j in range(n_lt)],
                axis=1).astype(BF16)
            o = _dot(p, v_ref[c * tk:(c + 1) * tk, v_lo:v_lo + v_w])
            if acc[i] is None:
                acc[i] = o
            else:
                alpha = jnp.exp2(m[i] - m_new)
                acc[i] = jnp.concatenate(
                    [acc[i][:, j * LANES:(j + 1) * LANES] * alpha for j in range(v_w // LANES)],
                    axis=1) + o
            m[i] = m_new
    return acc


def _diff_attn_kernel(q_ref, k_ref, v_ref, lam_ref, g_ref, o_ref, vx_ref, *, lam_init, tk):
    @pl.when(pl.program_id(2) == 0)
    def _():
        vx_ref[:, :LANES] = v_ref[...]
        vx_ref[:, LANES:] = jnp.ones((v_ref.shape[0], LANES), BF16)

    q = q_ref[...]
    lane = lax.broadcasted_iota(jnp.int32, q.shape, 1)
    zero = jnp.zeros_like(q)
    q1 = jnp.where(lane < DIFF_HEAD_DIM, q, zero)
    q2 = jnp.where(lane >= DIFF_HEAD_DIM, q, zero)
    acc1, acc2 = _attend_online(
        [(q1, k_ref, 0, vx_ref, 0, 2 * LANES), (q2, k_ref, 0, vx_ref, 0, 2 * LANES)], tk)
    a1 = acc1[:, :LANES] / acc1[:, LANES:]
    a2 = acc2[:, :LANES] / acc2[:, LANES:]

    lv = lam_ref[...]
    d1 = jnp.sum(lv[0:1] * lv[1:2], axis=-1, keepdims=True)
    d2 = jnp.sum(lv[2:3] * lv[3:4], axis=-1, keepdims=True)
    lam = jnp.exp(d1) - jnp.exp(d2) + lam_init
    o = a1 - lam * a2
    o_ref[...] = (_rms_norm(o, g_ref[...]) * (1.0 - lam_init)).astype(o_ref.dtype)


def _diff_attn_call(q, k, v, lam_vecs, subln_g, lam_init, batch, seq, tq=512, tk=512):
    n = q.shape[0]
    nq = seq // tq
    kern = functools.partial(_diff_attn_kernel, lam_init=lam_init, tk=tk)
    return pl.pallas_call(
        kern,
        out_shape=jax.ShapeDtypeStruct((n, DIFF_HEADS * LANES), BF16),
        grid=(batch, DIFF_HEADS, nq),
        in_specs=[pl.BlockSpec((tq, LANES), lambda b, h, i: (b * nq + i, h)),
                  pl.BlockSpec((seq, LANES), lambda b, h, i: (b, h)),
                  pl.BlockSpec((seq, LANES), lambda b, h, i: (b, h)),
                  pl.BlockSpec(lam_vecs.shape, lambda b, h, i: (0, 0)),
                  pl.BlockSpec((1, LANES), lambda b, h, i: (0, 0))],
        out_specs=pl.BlockSpec((tq, LANES), lambda b, h, i: (b * nq + i, h)),
        scratch_shapes=[pltpu.VMEM((seq, 2 * LANES), BF16)],
        compiler_params=_cparams("parallel", "parallel", "arbitrary"),
        name="diff_attn",
    )(q, k, v, lam_vecs, subln_g.reshape(1, LANES))


def _mla_attn_kernel(q_ref, k_ref, v_ref, o_ref, *, tk):
    acc_e, acc_o = _attend_online(
        [(q_ref[:, :LANES], k_ref, 0, v_ref, 0, LANES),
         (q_ref[:, LANES:], k_ref, LANES, v_ref, LANES, LANES)], tk)
    lane = lax.broadcasted_iota(jnp.int32, acc_e.shape, 1)
    lo = lane < MLA_V_DIM
    num = jnp.where(lo, acc_e, acc_o)
    den = jnp.where(lo, pltpu.roll(acc_e, MLA_V_DIM, 1), pltpu.roll(acc_o, MLA_V_DIM, 1))
    o_ref[...] = (num / den).astype(o_ref.dtype)


def _mla_attn_call(q, k, v, batch, seq, tq=512, tk=512):
    n = q.shape[0]
    nq = seq // tq
    n_pairs = MLA_HEADS // 2
    kern = functools.partial(_mla_attn_kernel, tk=tk)
    return pl.pallas_call(
        kern,
        out_shape=jax.ShapeDtypeStruct((n, n_pairs * LANES), BF16),
        grid=(batch, n_pairs, nq),
        in_specs=[pl.BlockSpec((tq, 2 * LANES), lambda b, h, i: (b * nq + i, h)),
                  pl.BlockSpec((seq, 2 * LANES), lambda b, h, i: (b, h)),
                  pl.BlockSpec((seq, 2 * LANES), lambda b, h, i: (b, h))],
        out_specs=pl.BlockSpec((tq, LANES), lambda b, h, i: (b * nq + i, h)),
        compiler_params=_cparams("parallel", "parallel", "arbitrary"),
        name="mla_attn",
    )(q, k, v)


def _na_window_mask():
    neg = np.full((len(NA_TABLES), NA_PAIR_Q, NA_BAND), -np.inf, np.float32)
    c = np.arange(GRID_W)[:, None]
    cp = np.arange(GRID_W)[None, :]
    wstart = np.clip(c - NA_WIN_COLS // 2, 0, GRID_W - NA_WIN_COLS)
    col_ok = (cp >= wstart) & (cp < wstart + NA_WIN_COLS)
    for t, rows in enumerate(NA_TABLES):
        for r, (_, pos) in enumerate(rows):
            for jb in range(NA_BAND // GRID_W):
                if 0 <= jb - pos < NA_WIN_ROWS:
                    neg[t, r * GRID_W:(r + 1) * GRID_W, jb * GRID_W:(jb + 1) * GRID_W] = np.where(col_ok, 0.0, -np.inf)
    return neg


def _na_bias_kernel(p_ref, neg_ref, o_ref):
    for t, rows in enumerate(NA_TABLES):
        for r, (dr0, pos) in enumerate(rows):
            for u in range(NA_BAND // LANES):
                dr = max(dr0 - pos + 2 * u, 0)
                src = p_ref[0, dr % 2, dr // 2:dr // 2 + 1, :]
                tile = pltpu.roll(jnp.broadcast_to(src, (GRID_W, LANES)),
                                  LANES - (NA_WIN_COLS - 1), 1, stride=1, stride_axis=0)
                qs = slice(r * GRID_W, (r + 1) * GRID_W)
                ks = slice(u * LANES, (u + 1) * LANES)
                o_ref[0, t, qs, ks] = tile + neg_ref[t, qs, ks]


def _na_bias_tables(rpb_l, neg):
    h, n_dr, n_dc = rpb_l.shape
    n_rows = 2 * (n_dr // 2 + 2)
    t_pad = jnp.zeros((h, n_rows, GRID_W), F32).at[:, :n_dr, :n_dc].set(rpb_l * LOG2E)
    p_even = t_pad.reshape(h, n_rows // 2, LANES)
    p_odd = jnp.concatenate([t_pad[:, 1:], jnp.zeros((h, 1, GRID_W), F32)], axis=1).reshape(h, n_rows // 2, LANES)
    pairs = jnp.stack([p_even, p_odd], axis=1)
    out = pl.pallas_call(
        _na_bias_kernel,
        out_shape=jax.ShapeDtypeStruct((h,) + neg.shape, F32),
        grid=(h,),
        in_specs=[pl.BlockSpec((1,) + pairs.shape[1:], lambda i: (i, 0, 0, 0)),
                  pl.BlockSpec(neg.shape, lambda i: (0, 0, 0))],
        out_specs=pl.BlockSpec((1,) + neg.shape, lambda i: (i, 0, 0, 0)),
        compiler_params=_cparams("parallel"),
        name="na_bias",
    )(pairs, neg)
    return out.reshape((h // 2, 2) + neg.shape)


def _na_kernel(q_ref, k_ref, v_ref, bias_ref, o_ref):
    rb = pl.program_id(2)
    n_rb = pl.num_programs(2)
    start = jnp.clip(NA_RB * rb - NA_WIN_ROWS // 2, 0, GRID_W - NA_SLAB_ROWS)
    first = rb == 0
    last = rb == n_rb - 1
    lane = lax.broadcasted_iota(jnp.int32, (NA_PAIR_Q, LANES), 1)
    lo = lane < NA_HEAD_DIM
    tiles = []
    for p in range(NA_RB // 2):
        off = jnp.where(first, NA_OFF_FIRST[p], jnp.where(last, NA_OFF_LAST[p], 2 * p))
        tab = jnp.where(first, NA_TAB_FIRST[p], jnp.where(last, NA_TAB_LAST[p], 0))
        tok0 = pl.multiple_of((start + off) * GRID_W, GRID_W)
        kb = k_ref[0, pl.ds(tok0, NA_BAND), :]
        qp = q_ref[0, p * NA_PAIR_Q:(p + 1) * NA_PAIR_Q, :]
        zero = jnp.zeros_like(qp)
        for hh in range(2):
            qm = jnp.where(lo if hh == 0 else jnp.logical_not(lo), qp, zero)
            tiles.append((tok0, _dot_nt(qm, kb) + bias_ref[0, hh, tab]))
    outs = []
    for tok0, s in tiles:
        m = jnp.max(s, axis=-1, keepdims=True)
        e = jnp.exp2(s - m)
        l = jnp.sum(e, axis=-1, keepdims=True)
        outs.append(_dot(e.astype(BF16), v_ref[0, pl.ds(tok0, NA_BAND), :]) / l)
    for p in range(NA_RB // 2):
        o_ref[0, p * NA_PAIR_Q:(p + 1) * NA_PAIR_Q, :] = jnp.where(
            lo, outs[2 * p], outs[2 * p + 1]).astype(o_ref.dtype)


def _na_call(q, k, v, bias, batch, seq):
    n_hp, n, _ = q.shape
    n_rb = seq // (NA_RB * GRID_W)
    blk_q = NA_RB * GRID_W
    return pl.pallas_call(
        _na_kernel,
        out_shape=jax.ShapeDtypeStruct((n_hp, n, LANES), BF16),
        grid=(n_hp, batch, n_rb),
        in_specs=[pl.BlockSpec((1, blk_q, LANES), lambda hp, b, r: (hp, b * n_rb + r, 0)),
                  pl.BlockSpec((1, seq, LANES), lambda hp, b, r: (hp, b, 0)),
                  pl.BlockSpec((1, seq, LANES), lambda hp, b, r: (hp, b, 0)),
                  pl.BlockSpec((1,) + bias.shape[1:], lambda hp, b, r: (hp, 0, 0, 0, 0))],
        out_specs=pl.BlockSpec((1, blk_q, LANES), lambda hp, b, r: (hp, b * n_rb + r, 0)),
        compiler_params=_cparams("parallel", "parallel", "arbitrary"),
        name="na_attn",
    )(q, k, v, bias)


def _merge_kernel(x_ref, na_ref, df_ref, ml_ref, wg_ref, bg_ref, wb_ref, wo_ref, g_ref, b_ref,
                  o_ref, *, alpha):
    x = x_ref[...]
    xb = x.astype(BF16)
    na = jnp.concatenate([na_ref[hp] for hp in range(na_ref.shape[0])], axis=1)
    branches = (na, df_ref[...], ml_ref[...])
    merged = None
    for br in range(N_BRANCH):
        z = _dot(xb, wg_ref[:, br * D_MODEL:(br + 1) * D_MODEL]) + bg_ref[:, br * D_MODEL:(br + 1) * D_MODEL]
        t = jax.nn.sigmoid(z) * _dot(branches[br], wb_ref[br])
        merged = t if merged is None else merged + t
    mix = _dot(merged.astype(BF16), wo_ref[...])
    o_ref[...] = _layer_norm(alpha * x + mix, g_ref[...], b_ref[...])


def _merge_call(x, na, df, ml, wg, bg, wb, wo, g, b, alpha, tm=512):
    n, d = x.shape
    n_hp = na.shape[0]
    row = lambda i: (i, 0)
    return pl.pallas_call(
        functools.partial(_merge_kernel, alpha=alpha),
        out_shape=jax.ShapeDtypeStruct((n, d), F32),
        grid=(n // tm,),
        in_specs=[pl.BlockSpec((tm, d), row),
                  pl.BlockSpec((n_hp, tm, LANES), lambda i: (0, i, 0)),
                  pl.BlockSpec((tm, BRANCH_WIDTH), row),
                  pl.BlockSpec((tm, BRANCH_WIDTH), row),
                  _resident(wg.shape), _resident((1, N_BRANCH * d)), _resident(wb.shape),
                  _resident(wo.shape), _resident((1, d)), _resident((1, d))],
        out_specs=pl.BlockSpec((tm, d), row),
        compiler_params=_cparams("parallel"),
        name="merge_ln",
    )(x, na, df, ml, wg, bg.reshape(1, -1), wb, wo, g.reshape(1, d), b.reshape(1, d))


def _ffn_kernel(x_ref, w1_ref, w2_ref, g_ref, b_ref, o_ref, *, alpha, d_ff, n_chunks):
    x = x_ref[...]
    xb = x.astype(BF16)
    fc = d_ff // n_chunks
    acc = None
    for c in range(n_chunks):
        gate = _dot(xb, w1_ref[:, c * fc:(c + 1) * fc])
        up = _dot(xb, w1_ref[:, d_ff + c * fc:d_ff + (c + 1) * fc])
        hid = (gate * jax.nn.sigmoid(gate) * up).astype(BF16)
        o = _dot(hid, w2_ref[c * fc:(c + 1) * fc, :])
        acc = o if acc is None else acc + o
    o_ref[...] = _layer_norm(alpha * x + acc, g_ref[...], b_ref[...])


def _ffn_call(x, w1, w2, g, b, alpha, tm=512, n_chunks=2):
    n, d = x.shape
    d_ff = w2.shape[0]
    row = lambda i: (i, 0)
    return pl.pallas_call(
        functools.partial(_ffn_kernel, alpha=alpha, d_ff=d_ff, n_chunks=n_chunks),
        out_shape=jax.ShapeDtypeStruct((n, d), F32),
        grid=(n // tm,),
        in_specs=[pl.BlockSpec((tm, d), row),
                  _resident(w1.shape), _resident(w2.shape), _resident((1, d)), _resident((1, d))],
        out_specs=pl.BlockSpec((tm, d), row),
        compiler_params=_cparams("parallel"),
        name="ffn_ln",
    )(x, w1, w2, g.reshape(1, d), b.reshape(1, d))


def _rope_tables(seq, rot_dim, lane_period, lane_lo):
    half = rot_dim // 2
    inv_freq = jnp.exp(-math.log(ROPE_THETA) * jnp.arange(half, dtype=F32) / half)
    ang = jnp.arange(seq, dtype=F32)[:, None] * inv_freq[None, :]
    cos, sin = jnp.cos(ang), jnp.sin(ang)
    d = (np.arange(LANES) % lane_period) - lane_lo
    is1 = (d >= 0) & (d < half)
    is2 = (d >= half) & (d < rot_dim)
    fidx = np.where(is1, d, np.where(is2, d - half, 0))
    cos_l, sin_l = cos[:, fidx], sin[:, fidx]
    c = jnp.where(is1 | is2, cos_l, 1.0)
    sa = jnp.where(is1, -sin_l, 0.0)
    sb = jnp.where(is2, sin_l, 0.0)
    return c, sa, sb


def _pad_cols(w, n_out, groups):
    out = jnp.zeros((w.shape[0], n_out), w.dtype)
    for dst, src, wd in groups:
        out = out.at[:, dst:dst + wd].set(w[:, src:src + wd])
    return out


def _layer_params(l, w_in, w_mla_qb, w_mla_kvb):
    d = w_in.shape[1]
    w = jnp.zeros((d, C_END), F32)
    w = w.at[:, :C_KR].set(w_in[l, :, :C_KR])
    w = w.at[:, C_KR + MLA_NOPE_DIM:C_KR + MLA_NOPE_DIM + MLA_ROPE_DIM].set(w_in[l, :, C_KR:N_PROJ])
    wg = w_in[l, :, N_PROJ:]
    qd = MLA_NOPE_DIM + MLA_ROPE_DIM
    n_pad = MLA_HEADS * LANES
    wq = _pad_cols(w_mla_qb[l], n_pad, [(h * LANES, h * qd, qd) for h in range(MLA_HEADS)])
    kvd = MLA_NOPE_DIM + MLA_V_DIM
    wk = _pad_cols(w_mla_kvb[l], n_pad, [(h * LANES, h * kvd, MLA_NOPE_DIM) for h in range(MLA_HEADS)])
    wv = _pad_cols(w_mla_kvb[l], n_pad,
                   [(h * LANES + (h % 2) * MLA_V_DIM, h * kvd + MLA_NOPE_DIM, MLA_V_DIM)
                    for h in range(MLA_HEADS)])
    return w.astype(BF16), wg.astype(BF16), wq.astype(BF16), wk.astype(BF16), wv.astype(BF16)


def _v_ones_row():
    lane = np.arange(MLA_HEADS * LANES)
    h, r = lane // LANES, lane % LANES
    ones = np.where(h % 2 == 0, r >= MLA_V_DIM, r < MLA_V_DIM)
    return jnp.asarray(ones.astype(np.float32)).reshape(1, -1)


def kernel(x, ln_in_g, ln_in_b, w_in, b_gate, na_rpb, diff_lambda, diff_subln_g, mla_q_norm_g,
           mla_kv_norm_g, w_mla_qb, w_mla_kvb, w_branch, w_out, ln1_g, ln1_b, w_ffn_in, w_ffn_out,
           ln2_g, ln2_b):
    batch, seq, d = x.shape
    depth = w_in.shape[0]
    assert d == D_MODEL and seq == GRID_W * GRID_W and w_in.shape[2] == N_PROJ + N_BRANCH * D_MODEL
    alpha = (2 * depth) ** 0.25

    dtabs = _rope_tables(seq, DIFF_ROT_DIM, DIFF_HEAD_DIM, 0)
    mtabs = _rope_tables(seq, MLA_ROPE_DIM, LANES, MLA_NOPE_DIM)
    vone = _v_ones_row()
    na_neg = jnp.asarray(_na_window_mask())

    h = _ln_call(x.reshape(batch * seq, d), ln_in_g, ln_in_b)
    for l in range(depth):
        lam_init = 0.8 - 0.6 * math.exp(-0.3 * l)
        w, wg, wq, wk, wv = _layer_params(l, w_in, w_mla_qb, w_mla_kvb)
        (naq, nak, nav, dfq, dfk, dfv, mlq, mlk, mlv) = _proj_call(
            h, w, wq, wk, wv, mla_q_norm_g[l].reshape(1, -1), mla_kv_norm_g[l].reshape(1, -1),
            vone, dtabs, mtabs, seq)
        na_out = _na_call(naq, nak, nav, _na_bias_tables(na_rpb[l], na_neg), batch, seq)
        df_out = _diff_attn_call(dfq, dfk, dfv, diff_lambda[l], diff_subln_g[l], lam_init, batch, seq)
        ml_out = _mla_attn_call(mlq, mlk, mlv, batch, seq)
        h = _merge_call(h, na_out, df_out, ml_out, wg, b_gate[l], w_branch[l].astype(BF16),
                        w_out[l].astype(BF16), ln1_g[l], ln1_b[l], alpha)
        h = _ffn_call(h, w_ffn_in[l].astype(BF16), w_ffn_out[l].astype(BF16), ln2_g[l], ln2_b[l], alpha)
    return h.reshape(batch, seq, d)
```

```python
import functools
import math

import numpy as np
import jax
import jax.numpy as jnp
from jax import lax
from jax.experimental import pallas as pl
from jax.experimental.pallas import tpu as pltpu

F32 = jnp.float32
BF16 = jnp.bfloat16

LANES = 128
VMEM_LIMIT_BYTES = 56 * 1024 * 1024

D_MODEL = 1024
GRID_W = 64
NA_HEADS = 8
NA_HEAD_DIM = 64
NA_WIN_ROWS = 8
NA_WIN_COLS = 16
DIFF_HEADS = 4
DIFF_HEAD_DIM = 64
DIFF_ROT_DIM = DIFF_HEAD_DIM // 4
MLA_HEADS = 8
MLA_NOPE_DIM = 64
MLA_ROPE_DIM = 32
MLA_V_DIM = 64
MLA_Q_RANK = 384
MLA_KV_RANK = 256
N_BRANCH = 3
BRANCH_WIDTH = 512
ROPE_THETA = 500000.0
LN_EPS = 1e-5
RMS_EPS = 1e-6
LOG2E = math.log2(math.e)

C_NAQ, C_NAK, C_NAV = 0, 512, 1024
C_DFQ, C_DFK, C_DFV = 1536, 2048, 2560
C_CQ = 3072
C_CKV = C_CQ + MLA_Q_RANK
C_KR = C_CKV + MLA_KV_RANK
C_END = C_KR + LANES
N_PROJ = 3 * 512 + 3 * 512 + MLA_Q_RANK + MLA_KV_RANK + MLA_ROPE_DIM

NA_RB = 8
NA_PAIR_Q = 2 * GRID_W
NA_BAND = 10 * GRID_W
NA_SLAB_ROWS = NA_RB + NA_WIN_ROWS
NA_TABLES = (((3, 0), (3, 1)),
             ((7, 0), (6, 0)),
             ((5, 0), (4, 0)),
             ((3, 2), (2, 2)),
             ((1, 2), (0, 2)))
NA_OFF_FIRST, NA_TAB_FIRST = (0, 0, 0, 2), (1, 2, 0, 0)
NA_OFF_LAST, NA_TAB_LAST = (4, 6, 6, 6), (0, 0, 3, 4)


def _cparams(*sem):
    return pltpu.CompilerParams(dimension_semantics=sem, vmem_limit_bytes=VMEM_LIMIT_BYTES)


def _resident(shape):
    nd = len(shape)
    return pl.BlockSpec(shape, lambda *_: (0,) * nd, pipeline_mode=pl.Buffered(1))


def _dot(a, b):
    return jnp.dot(a, b, preferred_element_type=F32)


def _dot_nt(a, b):
    return lax.dot_general(a, b, (((1,), (1,)), ((), ())), preferred_element_type=F32)


def _layer_norm(x, g, b):
    mu = jnp.mean(x, axis=-1, keepdims=True)
    xc = x - mu
    var = jnp.mean(xc * xc, axis=-1, keepdims=True)
    return xc * lax.rsqrt(var + LN_EPS) * g + b


def _rms_norm(x, g):
    return x * lax.rsqrt(jnp.mean(x * x, axis=-1, keepdims=True) + RMS_EPS) * g


def _ln_kernel(x_ref, g_ref, b_ref, o_ref):
    o_ref[...] = _layer_norm(x_ref[...], g_ref[...], b_ref[...])


def _ln_call(x, g, b, tm=512):
    n, d = x.shape
    return pl.pallas_call(
        _ln_kernel,
        out_shape=jax.ShapeDtypeStruct((n, d), F32),
        grid=(n // tm,),
        in_specs=[pl.BlockSpec((tm, d), lambda i: (i, 0)),
                  pl.BlockSpec((1, d), lambda i: (0, 0)),
                  pl.BlockSpec((1, d), lambda i: (0, 0))],
        out_specs=pl.BlockSpec((tm, d), lambda i: (i, 0)),
        compiler_params=_cparams("parallel"),
        name="ln_in",
    )(x, g.reshape(1, d), b.reshape(1, d))


def _rope_lanes(x, c, sa, sb, shift):
    return x * c + pltpu.roll(x, LANES - shift, 1) * sa + pltpu.roll(x, shift, 1) * sb


def _proj_kernel(x_ref, w_ref, wq_ref, wk_ref, wv_ref, qg_ref, kvg_ref, vone_ref,
                 dc_ref, dsa_ref, dsb_ref, mc_ref, msa_ref, msb_ref,
                 naq_ref, nak_ref, nav_ref, dfq_ref, dfk_ref, dfv_ref,
                 mlq_ref, mlk_ref, mlv_ref, *, na_qscale, df_qscale, ml_qscale):
    xb = x_ref[...].astype(BF16)

    def mm(lo, hi):
        return _dot(xb, w_ref[:, lo:hi])

    cq = _rms_norm(mm(C_CQ, C_CKV), qg_ref[...]).astype(BF16)
    ckv = _rms_norm(mm(C_CKV, C_KR), kvg_ref[...]).astype(BF16)
    kr = mm(C_KR, C_END)

    dc, dsa, dsb = dc_ref[...], dsa_ref[...], dsb_ref[...]
    half = DIFF_ROT_DIM // 2
    y = mm(C_DFQ, C_DFQ + 512)
    for h in range(DIFF_HEADS):
        blk = _rope_lanes(y[:, h * LANES:(h + 1) * LANES], dc, dsa, dsb, half)
        dfq_ref[:, h * LANES:(h + 1) * LANES] = (blk * df_qscale).astype(BF16)
    y = mm(C_DFK, C_DFK + 512)
    for h in range(DIFF_HEADS):
        blk = _rope_lanes(y[:, h * LANES:(h + 1) * LANES], dc, dsa, dsb, half)
        dfk_ref[:, h * LANES:(h + 1) * LANES] = blk.astype(BF16)

    mc, msa, msb = mc_ref[...], msa_ref[...], msb_ref[...]
    mhalf = MLA_ROPE_DIM // 2
    yq = _dot(cq, wq_ref[...])
    for h in range(MLA_HEADS):
        blk = _rope_lanes(yq[:, h * LANES:(h + 1) * LANES], mc, msa, msb, mhalf)
        mlq_ref[:, h * LANES:(h + 1) * LANES] = (blk * ml_qscale).astype(BF16)
    yk = _dot(ckv, wk_ref[...])
    for h in range(MLA_HEADS):
        blk = _rope_lanes(yk[:, h * LANES:(h + 1) * LANES] + kr, mc, msa, msb, mhalf)
        mlk_ref[:, h * LANES:(h + 1) * LANES] = blk.astype(BF16)
    mlv_ref[...] = (_dot(ckv, wv_ref[...]) + vone_ref[...]).astype(BF16)

    dfv_ref[...] = mm(C_DFV, C_DFV + 512).astype(BF16)
    n_hp = naq_ref.shape[0]
    y = mm(C_NAQ, C_NAQ + 512) * na_qscale
    for hp in range(n_hp):
        naq_ref[hp] = y[:, hp * LANES:(hp + 1) * LANES].astype(BF16)
    y = mm(C_NAK, C_NAK + 512)
    for hp in range(n_hp):
        nak_ref[hp] = y[:, hp * LANES:(hp + 1) * LANES].astype(BF16)
    y = mm(C_NAV, C_NAV + 512)
    for hp in range(n_hp):
        nav_ref[hp] = y[:, hp * LANES:(hp + 1) * LANES].astype(BF16)


def _proj_call(x, w, wq, wk, wv, qg, kvg, vone, dtabs, mtabs, seq, tm=512):
    n, d = x.shape
    n_pos_blocks = seq // tm
    row = lambda i: (i, 0)
    pos = lambda i: (i % n_pos_blocks, 0)
    hp3 = lambda i: (0, i, 0)
    n_hp = NA_HEADS * NA_HEAD_DIM // LANES
    kern = functools.partial(
        _proj_kernel,
        na_qscale=NA_HEAD_DIM ** -0.5 * LOG2E,
        df_qscale=DIFF_HEAD_DIM ** -0.5 * LOG2E,
        ml_qscale=(MLA_NOPE_DIM + MLA_ROPE_DIM) ** -0.5 * LOG2E)
    tab_spec = pl.BlockSpec((tm, LANES), pos)
    na_shape = jax.ShapeDtypeStruct((n_hp, n, LANES), BF16)
    return pl.pallas_call(
        kern,
        out_shape=(na_shape, na_shape, na_shape,
                   jax.ShapeDtypeStruct((n, 512), BF16),
                   jax.ShapeDtypeStruct((n, 512), BF16),
                   jax.ShapeDtypeStruct((n, 512), BF16),
                   jax.ShapeDtypeStruct((n, MLA_HEADS * LANES), BF16),
                   jax.ShapeDtypeStruct((n, MLA_HEADS * LANES), BF16),
                   jax.ShapeDtypeStruct((n, MLA_HEADS * LANES), BF16)),
        grid=(n // tm,),
        in_specs=[pl.BlockSpec((tm, d), row),
                  _resident(w.shape), _resident(wq.shape), _resident(wk.shape), _resident(wv.shape),
                  _resident(qg.shape), _resident(kvg.shape), _resident(vone.shape),
                  tab_spec, tab_spec, tab_spec, tab_spec, tab_spec, tab_spec],
        out_specs=(pl.BlockSpec((n_hp, tm, LANES), hp3),
                   pl.BlockSpec((n_hp, tm, LANES), hp3),
                   pl.BlockSpec((n_hp, tm, LANES), hp3),
                   pl.BlockSpec((tm, 512), row),
                   pl.BlockSpec((tm, 512), row),
                   pl.BlockSpec((tm, 512), row),
                   pl.BlockSpec((tm, MLA_HEADS * LANES), row),
                   pl.BlockSpec((tm, MLA_HEADS * LANES), row),
                   pl.BlockSpec((tm, MLA_HEADS * LANES), row)),
        compiler_params=_cparams("parallel"),
        name="proj",
    )(x, w, wq, wk, wv, qg, kvg, vone, *dtabs, *mtabs)


def _attend_online(streams, tk):
    n_str = len(streams)
    tq = streams[0][0].shape[0]
    n_chunks = streams[0][1].shape[0] // tk
    n_lt = tk // LANES

    def scores(i, c):
        q, k_ref, k_lo = streams[i][:3]
        return _dot_nt(q, k_ref[c * tk:(c + 1) * tk, k_lo:k_lo + LANES])

    nxt = [scores(i, 0) for i in range(n_str)]
    m = [None] * n_str
    acc = [None] * n_str
    for c in range(n_chunks):
        for i in range(n_str):
            _, _, _, v_ref, v_lo, v_w = streams[i]
            s = nxt[i]
            if c + 1 < n_chunks:
                nxt[i] = scores(i, c + 1)
            mw = s[:, :LANES]
            for j in range(1, n_lt):
                mw = jnp.maximum(mw, s[:, j * LANES:(j + 1) * LANES])
            mc = jnp.broadcast_to(jnp.max(mw, axis=-1, keepdims=True), (tq, LANES))
            m_new = mc if m[i] is None else jnp.maximum(m[i], mc)
            p = jnp.concatenate(
                [jnp.exp2(s[:, j * LANES:(j + 1) * LANES] - m_new) for j in range(n_lt)],
                axis=1).astype(BF16)
            o = _dot(p, v_ref[c * tk:(c + 1) * tk, v_lo:v_lo + v_w])
            if acc[i] is None:
                acc[i] = o
            else:
                alpha = jnp.exp2(m[i] - m_new)
                acc[i] = jnp.concatenate(
                    [acc[i][:, j * LANES:(j + 1) * LANES] * alpha for j in range(v_w // LANES)],
                    axis=1) + o
            m[i] = m_new
    return acc


def _diff_attn_kernel(q_ref, k_ref, v_ref, lam_ref, g_ref, o_ref, vx_ref, *, lam_init, tk):
    @pl.when(pl.program_id(2) == 0)
    def _():
        vx_ref[:, :LANES] = v_ref[...]
        vx_ref[:, LANES:] = jnp.ones((v_ref.shape[0], LANES), BF16)

    q = q_ref[...]
    lane = lax.broadcasted_iota(jnp.int32, q.shape, 1)
    zero = jnp.zeros_like(q)
    q1 = jnp.where(lane < DIFF_HEAD_DIM, q, zero)
    q2 = jnp.where(lane >= DIFF_HEAD_DIM, q, zero)
    acc1, acc2 = _attend_online(
        [(q1, k_ref, 0, vx_ref, 0, 2 * LANES), (q2, k_ref, 0, vx_ref, 0, 2 * LANES)], tk)
    a1 = acc1[:, :LANES] / acc1[:, LANES:]
    a2 = acc2[:, :LANES] / acc2[:, LANES:]

    lv = lam_ref[...]
    d1 = jnp.sum(lv[0:1] * lv[1:2], axis=-1, keepdims=True)
    d2 = jnp.sum(lv[2:3] * lv[3:4], axis=-1, keepdims=True)
    lam = jnp.exp(d1) - jnp.exp(d2) + lam_init
    o = a1 - lam * a2
    o_ref[...] = (_rms_norm(o, g_ref[...]) * (1.0 - lam_init)).astype(o_ref.dtype)


def _diff_attn_call(q, k, v, lam_vecs, subln_g, lam_init, batch, seq, tq=512, tk=512):
    n = q.shape[0]
    nq = seq // tq
    kern = functools.partial(_diff_attn_kernel, lam_init=lam_init, tk=tk)
    return pl.pallas_call(
        kern,
        out_shape=jax.ShapeDtypeStruct((n, DIFF_HEADS * LANES), BF16),
        grid=(batch, DIFF_HEADS, nq),
        in_specs=[pl.BlockSpec((tq, LANES), lambda b, h, i: (b * nq + i, h)),
                  pl.BlockSpec((seq, LANES), lambda b, h, i: (b, h)),
                  pl.BlockSpec((seq, LANES), lambda b, h, i: (b, h)),
                  pl.BlockSpec(lam_vecs.shape, lambda b, h, i: (0, 0)),
                  pl.BlockSpec((1, LANES), lambda b, h, i: (0, 0))],
        out_specs=pl.BlockSpec((tq, LANES), lambda b, h, i: (b * nq + i, h)),
        scratch_shapes=[pltpu.VMEM((seq, 2 * LANES), BF16)],
        compiler_params=_cparams("parallel", "parallel", "arbitrary"),
        name="diff_attn",
    )(q, k, v, lam_vecs, subln_g.reshape(1, LANES))


def _mla_attn_kernel(q_ref, k_ref, v_ref, o_ref, *, tk):
    acc_e, acc_o = _attend_online(
        [(q_ref[:, :LANES], k_ref, 0, v_ref, 0, LANES),
         (q_ref[:, LANES:], k_ref, LANES, v_ref, LANES, LANES)], tk)
    lane = lax.broadcasted_iota(jnp.int32, acc_e.shape, 1)
    lo = lane < MLA_V_DIM
    num = jnp.where(lo, acc_e, acc_o)
    den = jnp.where(lo, pltpu.roll(acc_e, MLA_V_DIM, 1), pltpu.roll(acc_o, MLA_V_DIM, 1))
    o_ref[...] = (num / den).astype(o_ref.dtype)


def _mla_attn_call(q, k, v, batch, seq, tq=512, tk=512):
    n = q.shape[0]
    nq = seq // tq
    n_pairs = MLA_HEADS // 2
    kern = functools.partial(_mla_attn_kernel, tk=tk)
    return pl.pallas_call(
        kern,
        out_shape=jax.ShapeDtypeStruct((n, n_pairs * LANES), BF16),
        grid=(batch, n_pairs, nq),
        in_specs=[pl.BlockSpec((tq, 2 * LANES), lambda b, h, i: (b * nq + i, h)),
                  pl.BlockSpec((seq, 2 * LANES), lambda b, h, i: (b, h)),
                  pl.BlockSpec((seq, 2 * LANES), lambda b, h, i: (b, h))],
        out_specs=pl.BlockSpec((tq, LANES), lambda b, h, i: (b * nq + i, h)),
        compiler_params=_cparams("parallel", "parallel", "arbitrary"),
        name="mla_attn",
    )(q, k, v)


def _na_window_mask():
    neg = np.full((len(NA_TABLES), NA_PAIR_Q, NA_BAND), -np.inf, np.float32)
    c = np.arange(GRID_W)[:, None]
    cp = np.arange(GRID_W)[None, :]
    wstart = np.clip(c - NA_WIN_COLS // 2, 0, GRID_W - NA_WIN_COLS)
    col_ok = (cp >= wstart) & (cp < wstart + NA_WIN_COLS)
    for t, rows in enumerate(NA_TABLES):
        for r, (_, pos) in enumerate(rows):
            for jb in range(NA_BAND // GRID_W):
                if 0 <= jb - pos < NA_WIN_ROWS:
                    neg[t, r * GRID_W:(r + 1) * GRID_W, jb * GRID_W:(jb + 1) * GRID_W] = np.where(col_ok, 0.0, -np.inf)
    return neg


def _na_bias_kernel(p_ref, neg_ref, o_ref):
    for t, rows in enumerate(NA_TABLES):
        for r, (dr0, pos) in enumerate(rows):
            for u in range(NA_BAND // LANES):
                dr = max(dr0 - pos + 2 * u, 0)
                src = p_ref[0, dr % 2, dr // 2:dr // 2 + 1, :]
                tile = pltpu.roll(jnp.broadcast_to(src, (GRID_W, LANES)),
                                  LANES - (NA_WIN_COLS - 1), 1, stride=1, stride_axis=0)
                qs = slice(r * GRID_W, (r + 1) * GRID_W)
                ks = slice(u * LANES, (u + 1) * LANES)
                o_ref[0, t, qs, ks] = tile + neg_ref[t, qs, ks]


def _na_bias_tables(rpb_l, neg):
    h, n_dr, n_dc = rpb_l.shape
    n_rows = 2 * (n_dr // 2 + 2)
    t_pad = jnp.zeros((h, n_rows, GRID_W), F32).at[:, :n_dr, :n_dc].set(rpb_l * LOG2E)
    p_even = t_pad.reshape(h, n_rows // 2, LANES)
    p_odd = jnp.concatenate([t_pad[:, 1:], jnp.zeros((h, 1, GRID_W), F32)], axis=1).reshape(h, n_rows // 2, LANES)
    pairs = jnp.stack([p_even, p_odd], axis=1)
    out = pl.pallas_call(
        _na_bias_kernel,
        out_shape=jax.ShapeDtypeStruct((h,) + neg.shape, F32),
        grid=(h,),
        in_specs=[pl.BlockSpec((1,) + pairs.shape[1:], lambda i: (i, 0, 0, 0)),
                  pl.BlockSpec(neg.shape, lambda i: (0, 0, 0))],
        out_specs=pl.BlockSpec((1,) + neg.shape, lambda i: (i, 0, 0, 0)),
        compiler_params=_cparams("parallel"),
        name="na_bias",
    )(pairs, neg)
    return out.reshape((h // 2, 2) + neg.shape)


def _na_kernel(q_ref, k_ref, v_ref, bias_ref, o_ref):
    rb = pl.program_id(2)
    n_rb = pl.num_programs(2)
    start = jnp.clip(NA_RB * rb - NA_WIN_ROWS // 2, 0, GRID_W - NA_SLAB_ROWS)
    first = rb == 0
    last = rb == n_rb - 1
    lane = lax.broadcasted_iota(jnp.int32, (NA_PAIR_Q, LANES), 1)
    lo = lane < NA_HEAD_DIM
    tiles = []
    for p in range(NA_RB // 2):
        off = jnp.where(first, NA_OFF_FIRST[p], jnp.where(last, NA_OFF_LAST[p], 2 * p))
        tab = jnp.where(first, NA_TAB_FIRST[p], jnp.where(last, NA_TAB_LAST[p], 0))
        tok0 = pl.multiple_of((start + off) * GRID_W, GRID_W)
        kb = k_ref[0, pl.ds(tok0, NA_BAND), :]
        qp = q_ref[0, p * NA_PAIR_Q:(p + 1) * NA_PAIR_Q, :]
        zero = jnp.zeros_like(qp)
        for hh in range(2):
            qm = jnp.where(lo if hh == 0 else jnp.logical_not(lo), qp, zero)
            tiles.append((tok0, _dot_nt(qm, kb) + bias_ref[0, hh, tab]))
    outs = []
    for tok0, s in tiles:
        m = jnp.max(s, axis=-1, keepdims=True)
        e = jnp.exp2(s - m)
        l = jnp.sum(e, axis=-1, keepdims=True)
        outs.append(_dot(e.astype(BF16), v_ref[0, pl.ds(tok0, NA_BAND), :]) / l)
    for p in range(NA_RB // 2):
        o_ref[0, p * NA_PAIR_Q:(p + 1) * NA_PAIR_Q, :] = jnp.where(
            lo, outs[2 * p], outs[2 * p + 1]).astype(o_ref.dtype)


def _na_call(q, k, v, bias, batch, seq):
    n_hp, n, _ = q.shape
    n_rb = seq // (NA_RB * GRID_W)
    blk_q = NA_RB * GRID_W
    return pl.pallas_call(
        _na_kernel,
        out_shape=jax.ShapeDtypeStruct((n_hp, n, LANES), BF16),
        grid=(n_hp, batch, n_rb),
        in_specs=[pl.BlockSpec((1, blk_q, LANES), lambda hp, b, r: (hp, b * n_rb + r, 0)),
                  pl.BlockSpec((1, seq, LANES), lambda hp, b, r: (hp, b, 0)),
                  pl.BlockSpec((1, seq, LANES), lambda hp, b, r: (hp, b, 0)),
                  pl.BlockSpec((1,) + bias.shape[1:], lambda hp, b, r: (hp, 0, 0, 0, 0))],
        out_specs=pl.BlockSpec((1, blk_q, LANES), lambda hp, b, r: (hp, b * n_rb + r, 0)),
        compiler_params=_cparams("parallel", "parallel", "arbitrary"),
        name="na_attn",
    )(q, k, v, bias)


def _merge_kernel(x_ref, na_ref, df_ref, ml_ref, wg_ref, bg_ref, wb_ref, wo_ref, g_ref, b_ref,
                  o_ref, *, alpha):
    x = x_ref[...]
    xb = x.astype(BF16)
    na = jnp.concatenate([na_ref[hp] for hp in range(na_ref.shape[0])], axis=1)
    branches = (na, df_ref[...], ml_ref[...])
    merged = None
    for br in range(N_BRANCH):
        z = _dot(xb, wg_ref[:, br * D_MODEL:(br + 1) * D_MODEL]) + bg_ref[:, br * D_MODEL:(br + 1) * D_MODEL]
        t = jax.nn.sigmoid(z) * _dot(branches[br], wb_ref[br])
        merged = t if merged is None else merged + t
    mix = _dot(merged.astype(BF16), wo_ref[...])
    o_ref[...] = _layer_norm(alpha * x + mix, g_ref[...], b_ref[...])


def _merge_call(x, na, df, ml, wg, bg, wb, wo, g, b, alpha, tm=512):
    n, d = x.shape
    n_hp = na.shape[0]
    row = lambda i: (i, 0)
    return pl.pallas_call(
        functools.partial(_merge_kernel, alpha=alpha),
        out_shape=jax.ShapeDtypeStruct((n, d), F32),
        grid=(n // tm,),
        in_specs=[pl.BlockSpec((tm, d), row),
                  pl.BlockSpec((n_hp, tm, LANES), lambda i: (0, i, 0)),
                  pl.BlockSpec((tm, BRANCH_WIDTH), row),
                  pl.BlockSpec((tm, BRANCH_WIDTH), row),
                  _resident(wg.shape), _resident((1, N_BRANCH * d)), _resident(wb.shape),
                  _resident(wo.shape), _resident((1, d)), _resident((1, d))],
        out_specs=pl.BlockSpec((tm, d), row),
        compiler_params=_cparams("parallel"),
        name="merge_ln",
    )(x, na, df, ml, wg, bg.reshape(1, -1), wb, wo, g.reshape(1, d), b.reshape(1, d))


def _ffn_kernel(x_ref, w1_ref, w2_ref, g_ref, b_ref, o_ref, *, alpha, d_ff, n_chunks):
    x = x_ref[...]
    xb = x.astype(BF16)
    fc = d_ff // n_chunks
    acc = None
    for c in range(n_chunks):
        gate = _dot(xb, w1_ref[:, c * fc:(c + 1) * fc])
        up = _dot(xb, w1_ref[:, d_ff + c * fc:d_ff + (c + 1) * fc])
        hid = (gate * jax.nn.sigmoid(gate) * up).astype(BF16)
        o = _dot(hid, w2_ref[c * fc:(c + 1) * fc, :])
        acc = o if acc is None else acc + o
    o_ref[...] = _layer_norm(alpha * x + acc, g_ref[...], b_ref[...])


def _ffn_call(x, w1, w2, g, b, alpha, tm=512, n_chunks=2):
    n, d = x.shape
    d_ff = w2.shape[0]
    row = lambda i: (i, 0)
    return pl.pallas_call(
        functools.partial(_ffn_kernel, alpha=alpha, d_ff=d_ff, n_chunks=n_chunks),
        out_shape=jax.ShapeDtypeStruct((n, d), F32),
        grid=(n // tm,),
        in_specs=[pl.BlockSpec((tm, d), row),
                  _resident(w1.shape), _resident(w2.shape), _resident((1, d)), _resident((1, d))],
        out_specs=pl.BlockSpec((tm, d), row),
        compiler_params=_cparams("parallel"),
        name="ffn_ln",
    )(x, w1, w2, g.reshape(1, d), b.reshape(1, d))


def _rope_tables(seq, rot_dim, lane_period, lane_lo):
    half = rot_dim // 2
    inv_freq = jnp.exp(-math.log(ROPE_THETA) * jnp.arange(half, dtype=F32) / half)
    ang = jnp.arange(seq, dtype=F32)[:, None] * inv_freq[None, :]
    cos, sin = jnp.cos(ang), jnp.sin(ang)
    d = (np.arange(LANES) % lane_period) - lane_lo
    is1 = (d >= 0) & (d < half)
    is2 = (d >= half) & (d < rot_dim)
    fidx = np.where(is1, d, np.where(is2, d - half, 0))
    cos_l, sin_l = cos[:, fidx], sin[:, fidx]
    c = jnp.where(is1 | is2, cos_l, 1.0)
    sa = jnp.where(is1, -sin_l, 0.0)
    sb = jnp.where(is2, sin_l, 0.0)
    return c, sa, sb


def _pad_cols(w, n_out, groups):
    out = jnp.zeros((w.shape[0], n_out), w.dtype)
    for dst, src, wd in groups:
        out = out.at[:, dst:dst + wd].set(w[:, src:src + wd])
    return out


def _layer_params(l, w_in, w_mla_qb, w_mla_kvb):
    d = w_in.shape[1]
    w = jnp.zeros((d, C_END), F32)
    w = w.at[:, :C_KR].set(w_in[l, :, :C_KR])
    w = w.at[:, C_KR + MLA_NOPE_DIM:C_KR + MLA_NOPE_DIM + MLA_ROPE_DIM].set(w_in[l, :, C_KR:N_PROJ])
    wg = w_in[l, :, N_PROJ:]
    qd = MLA_NOPE_DIM + MLA_ROPE_DIM
    n_pad = MLA_HEADS * LANES
    wq = _pad_cols(w_mla_qb[l], n_pad, [(h * LANES, h * qd, qd) for h in range(MLA_HEADS)])
    kvd = MLA_NOPE_DIM + MLA_V_DIM
    wk = _pad_cols(w_mla_kvb[l], n_pad, [(h * LANES, h * kvd, MLA_NOPE_DIM) for h in range(MLA_HEADS)])
    wv = _pad_cols(w_mla_kvb[l], n_pad,
                   [(h * LANES + (h % 2) * MLA_V_DIM, h * kvd + MLA_NOPE_DIM, MLA_V_DIM)
                    for h in range(MLA_HEADS)])
    return w.astype(BF16), wg.astype(BF16), wq.astype(BF16), wk.astype(BF16), wv.astype(BF16)


def _v_ones_row():
    lane = np.arange(MLA_HEADS * LANES)
    h, r = lane // LANES, lane % LANES
    ones = np.where(h % 2 == 0, r >= MLA_V_DIM, r < MLA_V_DIM)
    return jnp.asarray(ones.astype(np.float32)).reshape(1, -1)


def kernel(x, ln_in_g, ln_in_b, w_in, b_gate, na_rpb, diff_lambda, diff_subln_g, mla_q_norm_g,
           mla_kv_norm_g, w_mla_qb, w_mla_kvb, w_branch, w_out, ln1_g, ln1_b, w_ffn_in, w_ffn_out,
           ln2_g, ln2_b):
    batch, seq, d = x.shape
    depth = w_in.shape[0]
    assert d == D_MODEL and seq == GRID_W * GRID_W and w_in.shape[2] == N_PROJ + N_BRANCH * D_MODEL
    alpha = (2 * depth) ** 0.25

    dtabs = _rope_tables(seq, DIFF_ROT_DIM, DIFF_HEAD_DIM, 0)
    mtabs = _rope_tables(seq, MLA_ROPE_DIM, LANES, MLA_NOPE_DIM)
    vone = _v_ones_row()
    na_neg = jnp.asarray(_na_window_mask())

    h = _ln_call(x.reshape(batch * seq, d), ln_in_g, ln_in_b)
    for l in range(depth):
        lam_init = 0.8 - 0.6 * math.exp(-0.3 * l)
        w, wg, wq, wk, wv = _layer_params(l, w_in, w_mla_qb, w_mla_kvb)
        (naq, nak, nav, dfq, dfk, dfv, mlq, mlk, mlv) = _proj_call(
            h, w, wq, wk, wv, mla_q_norm_g[l].reshape(1, -1), mla_kv_norm_g[l].reshape(1, -1),
            vone, dtabs, mtabs, seq)
        na_out = _na_call(naq, nak, nav, _na_bias_tables(na_rpb[l], na_neg), batch, seq)
        df_out = _diff_attn_call(dfq, dfk, dfv, diff_lambda[l], diff_subln_g[l], lam_init, batch, seq)
        ml_out = _mla_attn_call(mlq, mlk, mlv, batch, seq)
        h = _merge_call(h, na_out, df_out, ml_out, wg, b_gate[l], w_branch[l].astype(BF16),
                        w_out[l].astype(BF16), ln1_g[l], ln1_b[l], alpha)
        h = _ffn_call(h, w_ffn_in[l].astype(BF16), w_ffn_out[l].astype(BF16), ln2_g[l], ln2_b[l], alpha)
    return h.reshape(batch, seq, d)
```
